```python
import math
import jax
import jax.numpy as jnp
from jax import lax
import numpy as np

D_MODEL = 1024
BATCH = 8
SEQ = 2048
DEPTH = 4
DEC_BATCH = 128
DEC_SEQ = 4
PAST_LEN = 8192
PAGE_SIZE = 128

N_META = 16
CHUNK = 128
NORM_EPS = 1e-6
NEG = -1e30
H_RET = 4
DK_RET = 64
DV_RET = 128
H_DIFF = 4
DH_DIFF = 64
DV_DIFF = 128
H_MLA = 8
D_NOPE = 64
D_ROPE = 32
DV_MLA = 64
D_CQ = 256
D_CKV = 256
ROPE_THETA = 10000.0
H_MLSTM = 4
DK_MLSTM = 128
DV_MLSTM = 128
D_FF = 2816
CONV_W = 3
N_EVEN = (DEPTH + 1) // 2
N_ODD = DEPTH // 2
EVEN_SPLITS = (H_RET * DK_RET, H_RET * DK_RET, H_RET * DV_RET, H_RET * DV_RET,
               H_DIFF * 2 * DH_DIFF, H_DIFF * 2 * DH_DIFF, H_DIFF * DV_DIFF)
E_IN = sum(EVEN_SPLITS)
E_MIX = H_RET * DV_RET + H_DIFF * DV_DIFF
ODD_SPLITS = (D_CQ, D_CKV, D_ROPE, H_MLSTM * DK_MLSTM, H_MLSTM * DK_MLSTM,
              H_MLSTM * DV_MLSTM, H_MLSTM * DV_MLSTM, H_MLSTM, H_MLSTM)
O_IN = sum(ODD_SPLITS)
O_MIX = H_MLA * DV_MLA + H_MLSTM * DV_MLSTM
F32 = jnp.float32

kernel_name = 'hybrid_retnet_diffattn_mla_mlstm_step'


def split_cols(z, sizes):
    return jnp.split(z, [int(i) for i in np.cumsum(sizes)[:-1]], axis=-1)


def rmsnorm(x, g):
    xf = x.astype(F32)
    y = xf * lax.rsqrt(jnp.mean(xf * xf, axis=-1, keepdims=True) + NORM_EPS)
    return (y * g.astype(F32)).astype(x.dtype)


def head_rmsnorm(x, g):
    xf = x.astype(F32)
    return xf * lax.rsqrt(jnp.mean(xf * xf, axis=-1, keepdims=True) + NORM_EPS) * g.astype(F32)


def head_layernorm(x, g):
    xf = x.astype(F32)
    xc = xf - jnp.mean(xf, axis=-1, keepdims=True)
    return xc * lax.rsqrt(jnp.mean(xc * xc, axis=-1, keepdims=True) + NORM_EPS) * g.astype(F32)


def rope(x, pos):
    half = D_ROPE // 2
    freq = ROPE_THETA ** (-jnp.arange(half, dtype=F32) / half)
    ang = pos.astype(F32)[:, None] * freq[None, :]
    cos = jnp.cos(ang)[None, :, None, :]
    sin = jnp.sin(ang)[None, :, None, :]
    x1 = x[..., :half].astype(F32)
    x2 = x[..., half:].astype(F32)
    return jnp.concatenate([x1 * cos - x2 * sin, x1 * sin + x2 * cos], axis=-1).astype(x.dtype)


def alibi_slopes(n_heads):
    return 2.0 ** (-8.0 * jnp.arange(1, n_heads + 1, dtype=F32) / n_heads)


def retention_log_decay(n_heads):
    return jnp.log(1.0 - 2.0 ** (-5.0 - jnp.arange(n_heads, dtype=F32)))


def lambda_init(layer):
    return 0.8 - 0.6 * math.exp(-0.3 * layer)


def diff_lambda(lq1, lk1, lq2, lk2, lam_init):
    return (jnp.exp(jnp.sum(lq1.astype(F32) * lk1.astype(F32)))
            - jnp.exp(jnp.sum(lq2.astype(F32) * lk2.astype(F32))) + lam_init)


def front_pad(x, pad, value=0.0):
    return jnp.pad(x, ((0, 0), (pad, 0)) + ((0, 0),) * (x.ndim - 2), constant_values=value)


def to_chunks(x):
    return jnp.moveaxis(x.reshape(x.shape[0], x.shape[1] // CHUNK, CHUNK, *x.shape[2:]), 1, 0)


def from_chunks(y):
    y = jnp.moveaxis(y, 0, 1)
    return y.reshape(y.shape[0], -1, *y.shape[3:])


def prompt_block_attention(q, k, v, slopes):
    B, L = q.shape[:2]
    pad = (-L) % CHUNK
    qp, kp, vp = front_pad(q, pad), front_pad(k, pad), front_pad(v, pad)
    lp = L + pad
    kpos = jnp.arange(lp) - pad
    scale = q.shape[-1] ** -0.5

    def block(i):
        qi = lax.dynamic_slice_in_dim(qp, i * CHUNK, CHUNK, axis=1)
        qpos = i * CHUNK + jnp.arange(CHUNK) - pad
        s = jnp.einsum('bqgmd,bkgmd->bgmqk', qi, kp, preferred_element_type=F32) * scale
        dist = (qpos[:, None] - kpos[None, :]).astype(F32)
        if slopes is not None:
            s = s - slopes[None, :, None, None, None] * dist
        mask = (kpos[None, :] >= 0) & (dist >= 0)
        p = jax.nn.softmax(jnp.where(mask, s, NEG), axis=-1)
        return jnp.einsum('bgmqk,bkgd->bqgmd', p.astype(v.dtype), vp)

    out = lax.map(block, jnp.arange(lp // CHUNK))
    return from_chunks(out)[:, pad:]


def online_softmax_update(carry, s, pv):
    m, l, acc = carry
    m_new = jnp.maximum(m, s.max(axis=-1))
    corr = jnp.exp(m - m_new)
    p = jnp.exp(s - m_new[..., None])
    return (m_new, l * corr + p.sum(axis=-1), acc * corr[..., None] + pv(p))


def diff_attention_sample(q, k_new, v_new, cache_k, cache_v, layer_idx, page_table, lam):
    DB, S = q.shape[:2]
    n_pages = page_table.shape[1]
    qpos = n_pages * PAGE_SIZE + jnp.arange(S)
    slopes = alibi_slopes(H_DIFF)
    qf = q.astype(F32)
    scale = DH_DIFF ** -0.5

    def scores(kblk, kpos):
        s = jnp.einsum('bqhmd,bkhmd->bhmqk', qf, kblk.astype(F32)) * scale
        return s - slopes[None, :, None, None, None] * (qpos[:, None] - kpos[None, :]).astype(F32)

    def pv(vblk):
        return lambda p: jnp.einsum('bhmqk,bkhd->bhmqd', p, vblk.astype(F32))

    init = (jnp.full((DB, H_DIFF, 2, S), NEG, F32), jnp.zeros((DB, H_DIFF, 2, S), F32),
            jnp.zeros((DB, H_DIFF, 2, S, DV_DIFF), F32))

    def step(carry, j):
        pages = page_table[:, j]
        kblk = cache_k[layer_idx, pages].reshape(DB, PAGE_SIZE, H_DIFF, 2, DH_DIFF)
        vblk = cache_v[layer_idx, pages]
        kpos = j * PAGE_SIZE + jnp.arange(PAGE_SIZE)
        return online_softmax_update(carry, scores(kblk, kpos), pv(vblk)), None

    carry, _ = lax.scan(step, init, jnp.arange(n_pages))
    causal = jnp.tril(jnp.ones((S, S), bool))
    s_new = jnp.where(causal, scores(k_new, qpos), NEG)
    _, l, acc = online_softmax_update(carry, s_new, pv(v_new))
    o = acc / l[..., None]
    o = o[:, :, 0] - lam * o[:, :, 1]
    return o.transpose(0, 2, 1, 3)


def mla_attention_sample(q_nope, q_pe, ckv_new, kpe_new, cache_ckv, cache_kpe, layer_idx, page_table, w_uk, w_uv):
    DB, S = q_nope.shape[:2]
    n_pages = page_table.shape[1]
    q_lat = jnp.einsum('bqhn,chn->bqhc', q_nope.astype(F32), w_uk.astype(F32))
    qpe = q_pe.astype(F32)
    scale = (D_NOPE + D_ROPE) ** -0.5

    def scores(ckv, kpe):
        return (jnp.einsum('bqhc,bkc->bhqk', q_lat, ckv.astype(F32))
                + jnp.einsum('bqhr,bkr->bhqk', qpe, kpe.astype(F32))) * scale

    def pv(ckv):
        return lambda p: jnp.einsum('bhqk,bkc->bhqc', p, ckv.astype(F32))

    init = (jnp.full((DB, H_MLA, S), NEG, F32), jnp.zeros((DB, H_MLA, S), F32),
            jnp.zeros((DB, H_MLA, S, D_CKV), F32))

    def step(carry, j):
        pages = page_table[:, j]
        ckv = cache_ckv[layer_idx, pages]
        kpe = cache_kpe[layer_idx, pages]
        return online_softmax_update(carry, scores(ckv, kpe), pv(ckv)), None

    carry, _ = lax.scan(step, init, jnp.arange(n_pages))
    causal = jnp.tril(jnp.ones((S, S), bool))
    s_new = jnp.where(causal, scores(ckv_new, kpe_new), NEG)
    _, l, acc = online_softmax_update(carry, s_new, pv(ckv_new))
    o_lat = acc / l[..., None]
    return jnp.einsum('bhqc,chd->bqhd', o_lat, w_uv.astype(F32))


def retention_chunk(s_prev, q, k, v, log_gamma):
    q, k, v = q.astype(F32), k.astype(F32), v.astype(F32)
    T = q.shape[1]
    t = jnp.arange(T, dtype=F32)
    dist = t[:, None] - t[None, :]
    decay = jnp.where(dist >= 0, jnp.exp(log_gamma[:, None, None] * jnp.maximum(dist, 0.0)), 0.0)
    qk = jnp.einsum('bihd,bjhd->bhij', q, k) * decay[None]
    inner = jnp.einsum('bhij,bjhv->bihv', qk, v)
    cross = jnp.einsum('bihd,bhdv->bihv', q, s_prev) * jnp.exp(log_gamma[None, :] * (t[:, None] + 1.0))[None, :, :, None]
    kw = k * jnp.exp(log_gamma[None, :] * (T - 1.0 - t)[:, None])[None, :, :, None]
    s_new = jnp.exp(log_gamma * T)[None, :, None, None] * s_prev + jnp.einsum('bjhd,bjhv->bhdv', kw, v)
    return s_new, inner + cross


def retention_prompt(q, k, v):
    B, L = q.shape[:2]
    pad = (-L) % CHUNK
    log_gamma = retention_log_decay(H_RET)
    xs = tuple(to_chunks(front_pad(a, pad)) for a in (q, k, v))
    s0 = jnp.zeros((B, H_RET, DK_RET, DV_RET), F32)
    s, o = lax.scan(lambda st, c: retention_chunk(st, c[0], c[1], c[2], log_gamma), s0, xs)
    return from_chunks(o)[:, pad:], s


def mlstm_chunk(state, q, k, v, logi, logf):
    c_prev, n_prev, m_prev = state
    q, k, v = q.astype(F32), k.astype(F32), v.astype(F32)
    T = q.shape[1]
    b = jnp.cumsum(logf, axis=1)
    causal = jnp.tril(jnp.ones((T, T), bool))
    d = b[:, :, None, :] - b[:, None, :, :] + logi[:, None, :, :]
    d = jnp.where(causal[None, :, :, None], d, NEG)
    inter = b + m_prev[:, None, :]
    m_q = jnp.maximum(inter, d.max(axis=2))
    w = jnp.exp(d - m_q[:, :, None, :])
    g = jnp.exp(inter - m_q)
    qk = jnp.einsum('bthd,bshd->btsh', q, k) * w
    num = jnp.einsum('btsh,bshv->bthv', qk, v) + g[..., None] * jnp.einsum('bthd,bhdv->bthv', q, c_prev)
    den = qk.sum(axis=2) + g * jnp.einsum('bthd,bhd->bth', q, n_prev)
    h = num / jnp.maximum(jnp.abs(den), jnp.exp(-m_q))[..., None]
    a = b[:, -1:, :] - b + logi
    m_new = jnp.maximum(b[:, -1] + m_prev, a.max(axis=1))
    ws = jnp.exp(a - m_new[:, None, :])
    gs = jnp.exp(b[:, -1] + m_prev - m_new)
    c_new = gs[..., None, None] * c_prev + jnp.einsum('bsh,bshd,bshv->bhdv', ws, k, v)
    n_new = gs[..., None] * n_prev + jnp.einsum('bsh,bshd->bhd', ws, k)
    return (c_new, n_new, m_new), h


def mlstm_prompt(q, k, v, logi, logf):
    B, L = q.shape[:2]
    pad = (-L) % CHUNK
    xs = (to_chunks(front_pad(q, pad)), to_chunks(front_pad(k, pad)), to_chunks(front_pad(v, pad)),
          to_chunks(front_pad(logi, pad, NEG)), to_chunks(front_pad(logf, pad)))
    state0 = (jnp.zeros((B, H_MLSTM, DK_MLSTM, DV_MLSTM), F32), jnp.zeros((B, H_MLSTM, DK_MLSTM), F32),
              jnp.zeros((B, H_MLSTM), F32))
    state, h = lax.scan(lambda st, c: mlstm_chunk(st, *c), state0, xs)
    return from_chunks(h)[:, pad:], state


def even_project(h, w_in):
    B, T = h.shape[:2]
    z = jnp.einsum('btd,de->bte', h, w_in)
    rq, rk, rv, rg, dq, dk, dv = split_cols(z, EVEN_SPLITS)
    rq = rq.reshape(B, T, H_RET, DK_RET)
    rk = rk.reshape(B, T, H_RET, DK_RET) * DK_RET ** -0.5
    rv = rv.reshape(B, T, H_RET, DV_RET)
    dq = dq.reshape(B, T, H_DIFF, 2, DH_DIFF)
    dk = dk.reshape(B, T, H_DIFF, 2, DH_DIFF)
    dv = dv.reshape(B, T, H_DIFF, DV_DIFF)
    return rq, rk, rv, rg, dq, dk, dv


def even_output(ret_o, rg, diff_o, ret_gn, diff_gn, lam_init, w_out):
    B, T = ret_o.shape[:2]
    gate = jax.nn.silu(rg.astype(F32)).reshape(B, T, H_RET, DV_RET)
    ret = head_layernorm(ret_o, ret_gn) * gate
    dif = head_rmsnorm(diff_o, diff_gn) * (1.0 - lam_init)
    mix = jnp.concatenate([ret.reshape(B, T, -1), dif.reshape(B, T, -1)], axis=-1).astype(w_out.dtype)
    return jnp.einsum('bte,ed->btd', mix, w_out)


def even_layer_prompt(h, w_in, w_out, ret_gn, diff_gn, lam, lam_init):
    B, T = h.shape[:2]
    rq, rk, rv, rg, dq, dk, dv = even_project(h, w_in)
    ret_o, s_new = retention_prompt(rq, rk, rv)
    att = prompt_block_attention(dq, dk, dv, alibi_slopes(H_DIFF))
    diff_o = att[:, :, :, 0].astype(F32) - lam * att[:, :, :, 1].astype(F32)
    y = even_output(ret_o, rg, diff_o, ret_gn, diff_gn, lam_init, w_out)
    return y, dk.reshape(B, T, H_DIFF, 2 * DH_DIFF), dv, s_new


def even_layer_sample(h, e, cache_k, cache_v, page_table, s_prev, w_in, w_out, ret_gn, diff_gn, lam, lam_init):
    B, T = h.shape[:2]
    rq, rk, rv, rg, dq, dk, dv = even_project(h, w_in)
    s_new, ret_o = retention_chunk(s_prev.astype(F32), rq, rk, rv, retention_log_decay(H_RET))
    diff_o = diff_attention_sample(dq, dk, dv, cache_k, cache_v, e, page_table, lam)
    y = even_output(ret_o, rg, diff_o, ret_gn, diff_gn, lam_init, w_out)
    return y, dk.reshape(B, T, H_DIFF, 2 * DH_DIFF), dv, s_new


def odd_project(h, pos, w_in, b_i, b_f, g_cq, g_ckv, w_uq):
    B, T = h.shape[:2]
    z = jnp.einsum('btd,de->bte', h, w_in)
    cq, ckv, kpe, mq, mk, mv, mo, mi, mf = split_cols(z, ODD_SPLITS)
    cq = rmsnorm(cq, g_cq)
    ckv = rmsnorm(ckv, g_ckv)
    qf = jnp.einsum('btc,chd->bthd', cq, w_uq)
    q_nope = qf[..., :D_NOPE]
    q_pe = rope(qf[..., D_NOPE:], pos)
    kpe = rope(kpe[:, :, None, :], pos)[:, :, 0]
    mq = mq.reshape(B, T, H_MLSTM, DK_MLSTM)
    mk = mk.reshape(B, T, H_MLSTM, DK_MLSTM) * DK_MLSTM ** -0.5
    mv = mv.reshape(B, T, H_MLSTM, DV_MLSTM)
    mo = mo.reshape(B, T, H_MLSTM, DV_MLSTM)
    logi = (mi + b_i).astype(F32)
    logf = jax.nn.log_sigmoid((mf + b_f).astype(F32))
    return q_nope, q_pe, ckv, kpe, mq, mk, mv, mo, logi, logf


def odd_output(mla_o, h_m, mo, mlstm_gn, w_out):
    B, T = mla_o.shape[:2]
    hm = head_layernorm(jax.nn.sigmoid(mo.astype(F32)) * h_m, mlstm_gn)
    mix = jnp.concatenate([mla_o.reshape(B, T, -1).astype(F32), hm.reshape(B, T, -1)], axis=-1).astype(w_out.dtype)
    return jnp.einsum('bte,ed->btd', mix, w_out)


def odd_layer_prompt(h, pos, w_in, w_out, b_i, b_f, g_cq, g_ckv, w_uq, w_uk, w_uv, mlstm_gn):
    B, T = h.shape[:2]
    q_nope, q_pe, ckv, kpe, mq, mk, mv, mo, logi, logf = odd_project(h, pos, w_in, b_i, b_f, g_cq, g_ckv, w_uq)
    k_nope = jnp.einsum('btc,chn->bthn', ckv, w_uk)
    v = jnp.einsum('btc,chd->bthd', ckv, w_uv)
    q = jnp.concatenate([q_nope, q_pe], axis=-1)[:, :, :, None]
    k = jnp.concatenate([k_nope, jnp.broadcast_to(kpe[:, :, None, :], (B, T, H_MLA, D_ROPE))], axis=-1)[:, :, :, None]
    mla_o = prompt_block_attention(q, k, v, None)[:, :, :, 0]
    h_m, state = mlstm_prompt(mq, mk, mv, logi, logf)
    y = odd_output(mla_o, h_m, mo, mlstm_gn, w_out)
    return y, ckv, kpe, state


def odd_layer_sample(h, pos, od, cache_ckv, cache_kpe, page_table, c_prev, n_prev, m_prev,
                     w_in, w_out, b_i, b_f, g_cq, g_ckv, w_uq, w_uk, w_uv, mlstm_gn):
    q_nope, q_pe, ckv, kpe, mq, mk, mv, mo, logi, logf = odd_project(h, pos, w_in, b_i, b_f, g_cq, g_ckv, w_uq)
    mla_o = mla_attention_sample(q_nope, q_pe, ckv, kpe, cache_ckv, cache_kpe, od, page_table, w_uk, w_uv)
    state, h_m = mlstm_chunk((c_prev.astype(F32), n_prev.astype(F32), m_prev.astype(F32)), mq, mk, mv, logi, logf)
    y = odd_output(mla_o, h_m, mo, mlstm_gn, w_out)
    return y, ckv, kpe, state


def conv_ffn(h, buf, w_gate, w_up, conv_w, conv_b, w_down):
    T = h.shape[1]
    g = jnp.einsum('btd,df->btf', h, w_gate)
    gp = jnp.concatenate([buf.astype(g.dtype), g], axis=1)
    gc = conv_b
    for j in range(CONV_W):
        gc = gc + conv_w[j] * gp[:, j:j + T]
    a = jax.nn.silu(gc.astype(F32)) * jnp.einsum('btd,df->btf', h, w_up).astype(F32)
    return jnp.einsum('btf,fd->btd', a.astype(h.dtype), w_down), gp[:, T:]


def setup_inputs(seed: int = 0) -> dict:
    key = jax.random.key(seed)
    keys = jax.random.split(key, 48)
    counter = [0]

    def nk():
        counter[0] += 1
        return keys[counter[0] - 1]

    def normal(shape, scale=1.0):
        return scale * jax.random.normal(nk(), shape, F32)

    def gain(shape):
        return 1.0 + 0.02 * normal(shape)

    n_pages = PAST_LEN // PAGE_SIZE
    n_pool = (DEC_BATCH * n_pages * 5) // 4
    x_prompt = normal((BATCH, SEQ, D_MODEL))
    x_sample = normal((DEC_BATCH, DEC_SEQ, D_MODEL))
    cache_diff_k = normal((N_EVEN, n_pool, PAGE_SIZE, H_DIFF, 2 * DH_DIFF))
    cache_diff_v = normal((N_EVEN, n_pool, PAGE_SIZE, H_DIFF, DV_DIFF))
    cache_mla_ckv = normal((N_ODD, n_pool, PAGE_SIZE, D_CKV))
    cache_mla_kpe = normal((N_ODD, n_pool, PAGE_SIZE, D_ROPE))
    page_table = jax.random.permutation(nk(), n_pool)[:DEC_BATCH * n_pages].reshape(DEC_BATCH, n_pages).astype(jnp.int32)
    state_ret = normal((N_EVEN, DEC_BATCH, H_RET, DK_RET, DV_RET), 0.1)
    state_mlstm_c = normal((N_ODD, DEC_BATCH, H_MLSTM, DK_MLSTM, DV_MLSTM))
    state_mlstm_n = normal((N_ODD, DEC_BATCH, H_MLSTM, DK_MLSTM))
    state_mlstm_m = normal((N_ODD, DEC_BATCH, H_MLSTM))
    state_ffn_conv = normal((DEPTH, DEC_BATCH, CONV_W - 1, D_FF))
    return {
        'x_prompt': x_prompt,
        'x_sample': x_sample,
        'cache_diff_k': cache_diff_k,
        'cache_diff_v': cache_diff_v,
        'cache_mla_ckv': cache_mla_ckv,
        'cache_mla_kpe': cache_mla_kpe,
        'page_table': page_table,
        'state_ret': state_ret,
        'state_mlstm_c': state_mlstm_c,
        'state_mlstm_n': state_mlstm_n,
        'state_mlstm_m': state_mlstm_m,
        'state_ffn_conv': state_ffn_conv,
        'meta_tokens': normal((N_META, D_MODEL)),
        'norm_mix': gain((DEPTH, D_MODEL)),
        'norm_ffn': gain((DEPTH, D_MODEL)),
        'norm_final': gain((D_MODEL,)),
        'w_in_even': normal((N_EVEN, D_MODEL, E_IN), D_MODEL ** -0.5),
        'w_out_even': normal((N_EVEN, E_MIX, D_MODEL), E_MIX ** -0.5),
        'ret_gn': gain((N_EVEN, H_RET, DV_RET)),
        'diff_gn': gain((N_EVEN, H_DIFF, DV_DIFF)),
        'lam_q1': normal((N_EVEN, DH_DIFF), 0.1),
        'lam_k1': normal((N_EVEN, DH_DIFF), 0.1),
        'lam_q2': normal((N_EVEN, DH_DIFF), 0.1),
        'lam_k2': normal((N_EVEN, DH_DIFF), 0.1),
        'w_in_odd': normal((N_ODD, D_MODEL, O_IN), D_MODEL ** -0.5),
        'w_out_odd': normal((N_ODD, O_MIX, D_MODEL), O_MIX ** -0.5),
        'mla_g_cq': gain((N_ODD, D_CQ)),
        'mla_g_ckv': gain((N_ODD, D_CKV)),
        'mla_w_uq': normal((N_ODD, D_CQ, H_MLA, D_NOPE + D_ROPE), D_CQ ** -0.5),
        'mla_w_uk': normal((N_ODD, D_CKV, H_MLA, D_NOPE), D_CKV ** -0.5),
        'mla_w_uv': normal((N_ODD, D_CKV, H_MLA, DV_MLA), D_CKV ** -0.5),
        'mlstm_b_i': normal((N_ODD, H_MLSTM), 0.1),
        'mlstm_b_f': jnp.linspace(3.0, 6.0, H_MLSTM, dtype=F32)[None, :] + normal((N_ODD, H_MLSTM), 0.1),
        'mlstm_gn': gain((N_ODD, H_MLSTM, DV_MLSTM)),
        'ffn_w_gate': normal((DEPTH, D_MODEL, D_FF), D_MODEL ** -0.5),
        'ffn_w_up': normal((DEPTH, D_MODEL, D_FF), D_MODEL ** -0.5),
        'ffn_conv_w': normal((DEPTH, CONV_W, D_FF), CONV_W ** -0.5),
        'ffn_conv_b': normal((DEPTH, D_FF), 0.02),
        'ffn_w_down': normal((DEPTH, D_FF, D_MODEL), D_FF ** -0.5),
    }


def reference(x_prompt, x_sample, cache_diff_k, cache_diff_v, cache_mla_ckv, cache_mla_kpe, page_table,
              state_ret, state_mlstm_c, state_mlstm_n, state_mlstm_m, state_ffn_conv,
              meta_tokens, norm_mix, norm_ffn, norm_final,
              w_in_even, w_out_even, ret_gn, diff_gn, lam_q1, lam_k1, lam_q2, lam_k2,
              w_in_odd, w_out_odd, mla_g_cq, mla_g_ckv, mla_w_uq, mla_w_uk, mla_w_uv,
              mlstm_b_i, mlstm_b_f, mlstm_gn,
              ffn_w_gate, ffn_w_up, ffn_conv_w, ffn_conv_b, ffn_w_down):
    dt = x_prompt.dtype
    B = x_prompt.shape[0]
    xp = jnp.concatenate([jnp.broadcast_to(meta_tokens.astype(dt)[None], (B, N_META, D_MODEL)), x_prompt], axis=1)
    xs = x_sample
    pos_p = jnp.arange(xp.shape[1])
    pos_s = page_table.shape[1] * PAGE_SIZE + jnp.arange(xs.shape[1])
    dk_p, dv_p, dk_s, dv_s, ret_p, ret_s = [], [], [], [], [], []
    ckv_p, kpe_p, ckv_s, kpe_s = [], [], [], []
    mc_p, mn_p, mm_p, mc_s, mn_s, mm_s = [], [], [], [], [], []
    cb_p, cb_s = [], []
    for layer in range(DEPTH):
        hp = rmsnorm(xp, norm_mix[layer])
        hs = rmsnorm(xs, norm_mix[layer])
        if layer % 2 == 0:
            e = layer // 2
            li = lambda_init(layer)
            lam = diff_lambda(lam_q1[e], lam_k1[e], lam_q2[e], lam_k2[e], li)
            yp, kp, vp, sp = even_layer_prompt(hp, w_in_even[e], w_out_even[e], ret_gn[e], diff_gn[e], lam, li)
            ys, ks, vs, ss = even_layer_sample(hs, e, cache_diff_k, cache_diff_v, page_table, state_ret[e],
                                               w_in_even[e], w_out_even[e], ret_gn[e], diff_gn[e], lam, li)
            dk_p.append(kp)
            dv_p.append(vp)
            dk_s.append(ks)
            dv_s.append(vs)
            ret_p.append(sp)
            ret_s.append(ss)
        else:
            od = layer // 2
            yp, cp, pp, stp = odd_layer_prompt(hp, pos_p, w_in_odd[od], w_out_odd[od], mlstm_b_i[od], mlstm_b_f[od],
                                               mla_g_cq[od], mla_g_ckv[od], mla_w_uq[od], mla_w_uk[od], mla_w_uv[od],
                                               mlstm_gn[od])
            ys, cs, ps, sts = odd_layer_sample(hs, pos_s, od, cache_mla_ckv, cache_mla_kpe, page_table,
                                               state_mlstm_c[od], state_mlstm_n[od], state_mlstm_m[od],
                                               w_in_odd[od], w_out_odd[od], mlstm_b_i[od], mlstm_b_f[od],
                                               mla_g_cq[od], mla_g_ckv[od], mla_w_uq[od], mla_w_uk[od], mla_w_uv[od],
                                               mlstm_gn[od])
            ckv_p.append(cp)
            kpe_p.append(pp)
            ckv_s.append(cs)
            kpe_s.append(ps)
            mc_p.append(stp[0])
            mn_p.append(stp[1])
            mm_p.append(stp[2])
            mc_s.append(sts[0])
            mn_s.append(sts[1])
            mm_s.append(sts[2])
        xp = xp + yp.astype(dt)
        xs = xs + ys.astype(dt)
        hp = rmsnorm(xp, norm_ffn[layer])
        hs = rmsnorm(xs, norm_ffn[layer])
        fp, bp = conv_ffn(hp, jnp.zeros((B, CONV_W - 1, D_FF), dt), ffn_w_gate[layer], ffn_w_up[layer],
                          ffn_conv_w[layer], ffn_conv_b[layer], ffn_w_down[layer])
        fs, bs = conv_ffn(hs, state_ffn_conv[layer], ffn_w_gate[layer], ffn_w_up[layer],
                          ffn_conv_w[layer], ffn_conv_b[layer], ffn_w_down[layer])
        cb_p.append(bp)
        cb_s.append(bs)
        xp = xp + fp.astype(dt)
        xs = xs + fs.astype(dt)
    y_prompt = rmsnorm(xp, norm_final)[:, N_META:]
    y_sample = rmsnorm(xs, norm_final)

    def st(lst):
        return jnp.stack(lst).astype(dt)

    return (y_prompt, y_sample,
            st(dk_p), st(dv_p), st(dk_s), st(dv_s),
            st(ckv_p), st(kpe_p), st(ckv_s), st(kpe_s),
            st(ret_p), st(ret_s),
            st(mc_p), st(mn_p), st(mm_p), st(mc_s), st(mn_s), st(mm_s),
            st(cb_p), st(cb_s))
```

```python
import functools
import math

import jax
import jax.numpy as jnp
import numpy as np
from jax import lax
from jax.experimental import pallas as pl
from jax.experimental.pallas import tpu as pltpu

F32 = jnp.float32
BF16 = jnp.bfloat16

N_META = 16
CHUNK = 128
NORM_EPS = 1e-6
NEG = -1e30
H_RET, DK_RET, DV_RET = 4, 64, 128
H_DIFF, DH_DIFF, DV_DIFF = 4, 64, 128
H_MLA, D_NOPE, D_ROPE, DV_MLA = 8, 64, 32, 64
D_CQ, D_CKV = 256, 256
ROPE_THETA = 10000.0
H_MLSTM, DK_MLSTM, DV_MLSTM = 4, 128, 128
CONV_W = 3
PAGE = 128
SROWS = 8
LANES = 128
HALO = 16
VMEM_LIMIT = 56 * 1024 * 1024

O_CQ, O_CKV, O_MQ, O_MK, O_MV, O_MO, O_GATE = 0, 256, 512, 1024, 1536, 2048, 2560
O_IN_PAD = 2688
G_KPA, G_KPB, G_MI, G_MF = 0, 32, 64, 68

RET_LOG_GAMMA = [float(np.log(np.float32(1.0) - np.float32(2.0) ** np.float32(-5.0 - h))) for h in range(H_RET)]
ALIBI = [float(2.0 ** (-8.0 * (h + 1) / H_DIFF)) for h in range(H_DIFF)]
MLA_SCALE = float((D_NOPE + D_ROPE) ** -0.5)


def _params(n_axes, vmem=None):
    return pltpu.CompilerParams(dimension_semantics=("arbitrary",) * n_axes,
                                vmem_limit_bytes=vmem or VMEM_LIMIT)


def _dot(a, b):
    return jnp.dot(a, b, preferred_element_type=F32)


def _dot_nt(a, b):
    return lax.dot_general(a, b, (((1,), (1,)), ((), ())), preferred_element_type=F32)


def _dot_tn(a, b):
    return lax.dot_general(a, b, (((0,), (0,)), ((), ())), preferred_element_type=F32)


def _rms(x, g):
    return x * lax.rsqrt(jnp.mean(x * x, axis=-1, keepdims=True) + NORM_EPS) * g


def _layernorm(x, g):
    xc = x - jnp.mean(x, axis=-1, keepdims=True)
    return xc * lax.rsqrt(jnp.mean(xc * xc, axis=-1, keepdims=True) + NORM_EPS) * g


def _sigmoid(x):
    return 1.0 / (1.0 + jnp.exp(-x))


def _log_sigmoid(x):
    return -(jnp.maximum(-x, 0.0) + jnp.log1p(jnp.exp(-jnp.abs(x))))


def _row_valid(tm, seq_len, pad, i):
    r = lax.broadcasted_iota(jnp.int32, (tm, 1), 0)
    if tm % seq_len == 0 and seq_len & (seq_len - 1) == 0:
        pos = r & (seq_len - 1)
    else:
        assert seq_len % tm == 0
        pos = r + lax.rem(i * tm, seq_len)
    return pos >= pad


def _norm_matmul_body(x_ref, g_ref, w_ref, o_ref, h_ref):
    @pl.when(pl.program_id(1) == 0)
    def _():
        h_ref[...] = _rms(x_ref[...], g_ref[...]).astype(BF16)

    o_ref[...] = _dot(h_ref[...], w_ref[...])


def norm_matmul(x, g, w, tm, tn):
    M, K = x.shape
    N = w.shape[1]
    return pl.pallas_call(
        _norm_matmul_body,
        grid=(M // tm, N // tn),
        in_specs=[pl.BlockSpec((tm, K), lambda i, j: (i, 0)),
                  pl.BlockSpec((1, K), lambda i, j: (0, 0)),
                  pl.BlockSpec((K, tn), lambda i, j: (0, j))],
        out_specs=pl.BlockSpec((tm, tn), lambda i, j: (i, j)),
        out_shape=jax.ShapeDtypeStruct((M, N), F32),
        scratch_shapes=[pltpu.VMEM((tm, K), BF16)],
        compiler_params=_params(2),
        name="norm_matmul",
    )(x, g.reshape(1, K), w)


def _proj_res_body(x_ref, a_ref, w_ref, o_ref, *, tm, seq_len, pad):
    y = x_ref[...] + _dot(a_ref[...], w_ref[...])
    o_ref[...] = jnp.where(_row_valid(tm, seq_len, pad, pl.program_id(0)), y, 0.0)


def proj_residual(x, a, w, tm, seq_len, pad):
    M, D = x.shape
    E = a.shape[1]
    return pl.pallas_call(
        functools.partial(_proj_res_body, tm=tm, seq_len=seq_len, pad=pad),
        grid=(M // tm,),
        in_specs=[pl.BlockSpec((tm, D), lambda i: (i, 0)),
                  pl.BlockSpec((tm, E), lambda i: (i, 0)),
                  pl.BlockSpec((E, D), lambda i: (0, 0))],
        out_specs=pl.BlockSpec((tm, D), lambda i: (i, 0)),
        out_shape=jax.ShapeDtypeStruct((M, D), F32),
        compiler_params=_params(1),
        name="proj_residual",
    )(x, a, w)


def _ffn_body(*refs, tm, rc, has_state):
    if has_state:
        x_ref, gn_ref, wg_ref, wu_ref, cw_ref, cb_ref, wd_ref, gs_ref, y_ref, conv_ref, h_ref = refs
    else:
        x_ref, gn_ref, wg_ref, wu_ref, cw_ref, cb_ref, wd_ref, y_ref, conv_ref, h_ref = refs
        gs_ref = None
    j = pl.program_id(1)
    n_chunks = tm // rc
    K = x_ref.shape[-1]

    @pl.when(j == 0)
    def _():
        h_ref[pl.ds(0, HALO), :] = jnp.zeros((HALO, K), BF16)

        def norm_chunk(c, carry):
            r0 = pl.multiple_of(c * rc, HALO)
            x = x_ref[pl.ds(r0, rc), :]
            h_ref[pl.ds(r0 + HALO, rc), :] = _rms(x, gn_ref[...]).astype(BF16)
            y_ref[pl.ds(r0, rc), :] = x
            return carry

        lax.fori_loop(0, n_chunks, norm_chunk, 0)

    cw = cw_ref[...]
    cb = cb_ref[...]

    def chunk(c, carry):
        r0 = pl.multiple_of(c * rc, HALO)
        hx = h_ref[pl.ds(r0, rc + HALO), :]
        g = _dot(hx, wg_ref[...])
        if has_state:
            g = g + gs_ref[...]
        u = _dot(hx[HALO:], wu_ref[...])
        g0 = g[HALO:]
        g1 = pltpu.roll(g, 1, 0)[HALO:]
        g2 = pltpu.roll(g, 2, 0)[HALO:]
        gc = cb + cw[0:1] * g2 + cw[1:2] * g1 + cw[2:3] * g0
        a = (gc * _sigmoid(gc)) * u
        y_ref[pl.ds(r0, rc), :] += _dot(a.astype(BF16), wd_ref[...])
        if has_state:
            conv_ref[...] = g0
        return carry

    lax.fori_loop(0, n_chunks, chunk, 0)
    if not has_state:
        tail = _dot(h_ref[pl.ds(tm, HALO), :], wg_ref[...])
        conv_ref[0] = tail[HALO - (CONV_W - 1):]


def conv_ffn(x, gn, wg, wu, cw, cb, wd, tm, rc, tf, gstate=None):
    M, D = x.shape
    F = wg.shape[1]
    has_state = gstate is not None
    if has_state:
        assert rc == tm == M
    in_specs = [pl.BlockSpec((tm, D), lambda i, j: (i, 0)),
                pl.BlockSpec((1, D), lambda i, j: (0, 0)),
                pl.BlockSpec((D, tf), lambda i, j: (0, j)),
                pl.BlockSpec((D, tf), lambda i, j: (0, j)),
                pl.BlockSpec((CONV_W, tf), lambda i, j: (0, j)),
                pl.BlockSpec((1, tf), lambda i, j: (0, j)),
                pl.BlockSpec((tf, D), lambda i, j: (j, 0))]
    args = [x, gn.reshape(1, D), wg, wu, cw, cb.reshape(1, F), wd]
    if has_state:
        in_specs.append(pl.BlockSpec((tm + HALO, tf), lambda i, j: (0, j)))
        args.append(gstate)
        conv_spec = pl.BlockSpec((tm, tf), lambda i, j: (0, j))
        conv_shape = jax.ShapeDtypeStruct((M, F), F32)
    else:
        conv_spec = pl.BlockSpec((1, CONV_W - 1, tf), lambda i, j: (i, 0, j))
        conv_shape = jax.ShapeDtypeStruct((M // tm, CONV_W - 1, F), F32)
    return pl.pallas_call(
        functools.partial(_ffn_body, tm=tm, rc=rc, has_state=has_state),
        grid=(M // tm, F // tf),
        in_specs=in_specs,
        out_specs=[pl.BlockSpec((tm, D), lambda i, j: (i, 0)), conv_spec],
        out_shape=[jax.ShapeDtypeStruct((M, D), F32), conv_shape],
        scratch_shapes=[pltpu.VMEM((tm + HALO, D), BF16)],
        compiler_params=_params(2),
        name="conv_ffn",
    )(*args)


def _final_norm_body(x_ref, g_ref, o_ref):
    o_ref[0] = _rms(x_ref[0], g_ref[...])


def final_norm(x3, g, row_off, rows_out, tr):
    B, L, D = x3.shape
    assert row_off % tr == 0 and rows_out % tr == 0
    ob = row_off // tr
    return pl.pallas_call(
        _final_norm_body,
        grid=(B, rows_out // tr),
        in_specs=[pl.BlockSpec((1, tr, D), lambda b, i: (b, i + ob, 0)),
                  pl.BlockSpec((1, D), lambda b, i: (0, 0))],
        out_specs=pl.BlockSpec((1, tr, D), lambda b, i: (b, i, 0)),
        out_shape=jax.ShapeDtypeStruct((B, rows_out, D), F32),
        compiler_params=_params(2),
        name="final_norm",
    )(x3, g.reshape(1, D))


def _retention_chunk(zr, s_pairs, t_off, t_eff):
    T = zr.shape[0]
    lane = lax.broadcasted_iota(jnp.int32, (1, LANES), 1)
    first = lane < DK_RET
    ti = lax.broadcasted_iota(jnp.int32, (T, T), 0)
    si = lax.broadcasted_iota(jnp.int32, (T, T), 1)
    dist = (ti - si).astype(F32)
    tau = lax.broadcasted_iota(jnp.int32, (T, 1), 0).astype(F32) - float(t_off)
    rowi = lax.broadcasted_iota(jnp.int32, (LANES, 1), 0)
    outs, new_pairs = [], []
    for p in range(H_RET // 2):
        qp = zr[:, p * 128:(p + 1) * 128]
        kp = zr[:, 256 + p * 128:256 + (p + 1) * 128] * (DK_RET ** -0.5)
        kb = kp.astype(BF16)
        sb = s_pairs[p].astype(BF16)
        upd = None
        for s in range(2):
            h = 2 * p + s
            lg = RET_LOG_GAMMA[h]
            msk = first if s == 0 else jnp.logical_not(first)
            qh = jnp.where(msk, qp, 0.0).astype(BF16)
            decay = jnp.where(dist >= 0, jnp.exp(lg * jnp.maximum(dist, 0.0)), 0.0)
            qk = (_dot_nt(qh, kb) * decay).astype(BF16)
            vh = zr[:, 512 + h * 128:512 + (h + 1) * 128].astype(BF16)
            inner = _dot(qk, vh)
            cross = _dot(qh, sb) * jnp.exp(lg * (tau + 1.0))
            outs.append(inner + cross)
            kw = jnp.where(msk, kp * jnp.exp(lg * (float(t_eff) - 1.0 - tau)), 0.0).astype(BF16)
            u = _dot_tn(kw, vh)
            upd = u if upd is None else upd + u
        g0 = float(np.exp(np.float32(RET_LOG_GAMMA[2 * p]) * np.float32(t_eff)))
        g1 = float(np.exp(np.float32(RET_LOG_GAMMA[2 * p + 1]) * np.float32(t_eff)))
        new_pairs.append(jnp.where(rowi < DK_RET, g0, g1) * s_pairs[p] + upd)
    return outs, new_pairs


def _retention_mix(outs, zr, rgn):
    res = []
    for h in range(H_RET):
        rg = zr[:, 1024 + h * 128:1024 + (h + 1) * 128]
        gate = rg * _sigmoid(rg)
        res.append(_layernorm(outs[h], rgn[h:h + 1]) * gate)
    return res


def _softmax_step(m, l, s):
    m_new = jnp.maximum(m, jnp.max(s, axis=-1, keepdims=True))
    corr = jnp.exp(m - m_new)
    p = jnp.exp(s - m_new)
    return m_new, l * corr + jnp.sum(p, axis=-1, keepdims=True), corr, p


def _diff_lambda(lam_ref, lam_init):
    lv = lam_ref[...]
    return (jnp.exp(jnp.sum(lv[0:1] * lv[1:2], axis=-1, keepdims=True))
            - jnp.exp(jnp.sum(lv[2:3] * lv[3:4], axis=-1, keepdims=True)) + lam_init)


def _mlstm_gates(gt, bvec, valid):
    gb = gt + bvec
    li = jnp.where(valid, gb, NEG)
    lf = jnp.where(valid, _log_sigmoid(gb), 0.0)
    row = lax.broadcasted_iota(jnp.int32, (CHUNK, 1), 0)
    sh = 1
    while sh < CHUNK:
        lf = lf + jnp.where(row >= sh, pltpu.roll(lf, sh, 0), 0.0)
        sh *= 2
    lane = lax.broadcasted_iota(jnp.int32, (1, LANES), 1)
    x = jnp.where(lane >= G_MF, lf, li)
    return x, x.T


def _mlstm_head(h, x, xt, q, k, v, cext, m_prev):
    bc = x[:, G_MF + h:G_MF + h + 1]
    br = xt[G_MF + h:G_MF + h + 1, :]
    lic = x[:, G_MI + h:G_MI + h + 1]
    lir = xt[G_MI + h:G_MI + h + 1, :]
    b_last = x[CHUNK - 1:CHUNK, G_MF + h:G_MF + h + 1]
    ti = lax.broadcasted_iota(jnp.int32, (CHUNK, CHUNK), 0)
    si = lax.broadcasted_iota(jnp.int32, (CHUNK, CHUNK), 1)
    d = jnp.where(si <= ti, bc - br + lir, NEG)
    inter = bc + m_prev
    m_q = jnp.maximum(inter, jnp.max(d, axis=-1, keepdims=True))
    w = jnp.exp(d - m_q)
    gq = jnp.exp(inter - m_q)
    qb = q.astype(BF16)
    kb = k.astype(BF16)
    vb = v.astype(BF16)
    qk = _dot_nt(qb, kb) * w
    r = _dot(qb, cext.astype(BF16))
    num = _dot(qk.astype(BF16), vb) + gq * r[:, :DV_MLSTM]
    den = jnp.sum(qk, axis=-1, keepdims=True) + gq * r[:, DV_MLSTM:DV_MLSTM + 1]
    h_out = num / jnp.maximum(jnp.abs(den), jnp.exp(-m_q))
    a = b_last - bc + lic
    m_new = jnp.maximum(b_last + m_prev, jnp.max(a, axis=0, keepdims=True))
    ws = jnp.exp(a - m_new)
    gs = jnp.exp(b_last + m_prev - m_new)
    kw = (k * ws).astype(BF16)
    vext = jnp.concatenate([vb, jnp.ones((CHUNK, LANES), BF16)], axis=1)
    return h_out, gs * cext + _dot_tn(kw, vext), m_new


def _mlstm_chunk(mq, mk, mv, mo, gt, bvec, valid, cexts, ms, gn):
    x, xt = _mlstm_gates(gt, bvec, valid)
    outs, ncs, nms = [], [], []
    for h in range(H_MLSTM):
        sl = slice(h * 128, (h + 1) * 128)
        ho, nc, nm = _mlstm_head(h, x, xt, mq[:, sl], mk[:, sl] * (DK_MLSTM ** -0.5), mv[:, sl],
                                 cexts[h], ms[h])
        outs.append(_layernorm(_sigmoid(mo[:, sl]) * ho, gn[h:h + 1]))
        ncs.append(nc)
        nms.append(nm)
    return outs, ncs, nms


def _pad_rows(a, rows):
    return jnp.concatenate([a, jnp.zeros((rows - a.shape[0], a.shape[1]), a.dtype)], axis=0)


def _even_prompt_body(zr_ref, zq_ref, zk_ref, zv_ref, rgn_ref, dgn_ref, lam_ref, mix_ref, st_ref, s_ref,
                      *, lam_init, pad, nc):
    i = pl.program_id(1)

    @pl.when(i == 0)
    def _():
        s_ref[...] = jnp.zeros_like(s_ref)

    zr = zr_ref[0]
    outs, new_pairs = _retention_chunk(zr, [s_ref[0], s_ref[1]], 0, CHUNK)
    s_ref[0] = new_pairs[0]
    s_ref[1] = new_pairs[1]
    ret = _retention_mix(outs, zr, rgn_ref[...])
    for h in range(H_RET):
        mix_ref[0, :, h * 128:(h + 1) * 128] = ret[h].astype(BF16)

    lam = _diff_lambda(lam_ref, lam_init)
    lane = lax.broadcasted_iota(jnp.int32, (1, LANES), 1)
    first = lane < DH_DIFF
    ti = lax.broadcasted_iota(jnp.int32, (CHUNK, CHUNK), 0)
    si = lax.broadcasted_iota(jnp.int32, (CHUNK, CHUNK), 1)
    d0 = (ti - si).astype(F32)
    dgn = dgn_ref[...]
    for g in range(H_DIFF):
        sl = slice(g * 128, (g + 1) * 128)
        qg = zq_ref[0, :, sl] * (DH_DIFF ** -0.5)
        q0 = jnp.where(first, qg, 0.0).astype(BF16)
        q1 = jnp.where(first, 0.0, qg).astype(BF16)
        slope = ALIBI[g]

        def body(j, carry, sl=sl, q0=q0, q1=q1, slope=slope):
            m0, l0, a0, m1, l1, a1 = carry
            r0 = pl.multiple_of(j * CHUNK, CHUNK)
            kj = zk_ref[0, pl.ds(r0, CHUNK), sl].astype(BF16)
            vj = zv_ref[0, pl.ds(r0, CHUNK), sl].astype(BF16)
            dist = d0 + ((i - j) * CHUNK).astype(F32)
            valid = jnp.logical_and(dist >= 0, si + j * CHUNK >= pad)
            bias = slope * dist
            s0 = jnp.where(valid, _dot_nt(q0, kj) - bias, NEG)
            s1 = jnp.where(valid, _dot_nt(q1, kj) - bias, NEG)
            m0, l0, c0, p0 = _softmax_step(m0, l0, s0)
            m1, l1, c1, p1 = _softmax_step(m1, l1, s1)
            a0 = a0 * c0 + _dot(p0.astype(BF16), vj)
            a1 = a1 * c1 + _dot(p1.astype(BF16), vj)
            return m0, l0, a0, m1, l1, a1

        mi = jnp.full((CHUNK, 1), NEG, F32)
        zl = jnp.zeros((CHUNK, 1), F32)
        za = jnp.zeros((CHUNK, DV_DIFF), F32)
        m0, l0, a0, m1, l1, a1 = lax.fori_loop(0, i + 1, body, (mi, zl, za, mi, zl, za))
        diff = a0 / l0 - lam * (a1 / l1)
        mix_ref[0, :, 512 + g * 128:512 + (g + 1) * 128] = (
            _rms(diff, dgn[g:g + 1]) * (1.0 - lam_init)).astype(BF16)

    @pl.when(i == nc - 1)
    def _():
        st_ref[0] = s_ref[...]


def even_prompt_mixer(z3, ret_gn, diff_gn, lam_vecs, lam_init, pad):
    B, Lp, _ = z3.shape
    nc = Lp // CHUNK
    return pl.pallas_call(
        functools.partial(_even_prompt_body, lam_init=lam_init, pad=pad, nc=nc),
        grid=(B, nc),
        in_specs=[pl.BlockSpec((1, CHUNK, 1536), lambda b, i: (b, i, 0)),
                  pl.BlockSpec((1, CHUNK, 512), lambda b, i: (b, i, 3)),
                  pl.BlockSpec((1, Lp, 512), lambda b, i: (b, 0, 4)),
                  pl.BlockSpec((1, Lp, 512), lambda b, i: (b, 0, 5)),
                  pl.BlockSpec((H_RET, DV_RET), lambda b, i: (0, 0)),
                  pl.BlockSpec((H_DIFF, DV_DIFF), lambda b, i: (0, 0)),
                  pl.BlockSpec((4, DH_DIFF), lambda b, i: (0, 0))],
        out_specs=[pl.BlockSpec((1, CHUNK, 1024), lambda b, i: (b, i, 0)),
                   pl.BlockSpec((1, 2, 128, 128), lambda b, i: (b, 0, 0, 0))],
        out_shape=[jax.ShapeDtypeStruct((B, Lp, 1024), BF16),
                   jax.ShapeDtypeStruct((B, 2, 128, 128), F32)],
        scratch_shapes=[pltpu.VMEM((2, 128, 128), F32)],
        compiler_params=_params(2),
        name="even_prompt_mixer",
    )(z3, z3, z3, z3, ret_gn, diff_gn, lam_vecs)


def _even_sample_body(pt_ref, zr_ref, zq_ref, zk_ref, zv_ref, st_in_ref, rgn_ref, dgn_ref, lam_ref, *rest,
                      P, n_steps, n_pages, lam_init, s_real):
    kp_refs = rest[:P]
    vp_refs = rest[P:2 * P]
    mix_ref, st_out_ref, qbd_ref, m_ref, l_ref, acc_ref = rest[2 * P:]
    j = pl.program_id(1)
    nrow = 2 * H_DIFF * SROWS
    row = lax.broadcasted_iota(jnp.int32, (nrow, 1), 0)
    lane = lax.broadcasted_iota(jnp.int32, (1, LANES), 1)
    rr = row & (SROWS - 1)
    slope = jnp.where(row < 16, ALIBI[0], jnp.where(row < 32, ALIBI[1], jnp.where(row < 48, ALIBI[2], ALIBI[3])))
    qpos = n_pages * PAGE + rr - (SROWS - s_real)

    @pl.when(j == 0)
    def _():
        dq = zq_ref[0] * (DH_DIFF ** -0.5)
        q64 = jnp.concatenate([dq] * (2 * H_DIFF), axis=0)
        col = lax.broadcasted_iota(jnp.int32, (1, 2 * H_DIFF * DH_DIFF), 1)
        blk = jnp.right_shift(col, 6) == jnp.right_shift(row, 3)
        qbd_ref[...] = jnp.where(blk, q64, 0.0).astype(BF16)
        m_ref[...] = jnp.full_like(m_ref, NEG)
        l_ref[...] = jnp.zeros_like(l_ref)
        acc_ref[...] = jnp.zeros_like(acc_ref)

    def attend(kb, vb, s_fix):
        s = s_fix(_dot_nt(qbd_ref[...], kb))
        m_new, l_new, corr, p = _softmax_step(m_ref[...], l_ref[...], s)
        m_ref[...] = m_new
        l_ref[...] = l_new
        acc_ref[...] = acc_ref[...] * corr + _dot(p.astype(BF16), vb)

    for r in range(P):
        kpos = (j * P + r) * PAGE + lane

        def past(s, kpos=kpos):
            return s - slope * (qpos - kpos).astype(F32)

        attend(kp_refs[r][...].astype(BF16), vp_refs[r][...].astype(BF16), past)

    @pl.when(j == n_steps - 1)
    def _():
        kn = _pad_rows(zk_ref[0], PAGE).astype(BF16)
        vn = _pad_rows(zv_ref[0], PAGE).astype(BF16)

        def new(s):
            valid = jnp.logical_and(lane >= SROWS - s_real, lane <= rr)
            return jnp.where(valid, s - slope * (rr - lane).astype(F32), NEG)

        attend(kn, vn, new)
        o = acc_ref[...] / l_ref[...]
        lam = _diff_lambda(lam_ref, lam_init)
        dgn = dgn_ref[...]
        for g in range(H_DIFF):
            o0 = o[16 * g:16 * g + 8, g * 128:(g + 1) * 128]
            o1 = o[16 * g + 8:16 * g + 16, g * 128:(g + 1) * 128]
            diff = o0 - lam * o1
            mix_ref[0, :, 512 + g * 128:512 + (g + 1) * 128] = (
                _rms(diff, dgn[g:g + 1]) * (1.0 - lam_init)).astype(BF16)
        zr = _pad_rows(zr_ref[0], CHUNK)
        outs, new_pairs = _retention_chunk(zr, [st_in_ref[0, 0], st_in_ref[0, 1]], SROWS - s_real, s_real)
        st_out_ref[0, 0] = new_pairs[0]
        st_out_ref[0, 1] = new_pairs[1]
        ret = _retention_mix([x[:SROWS] for x in outs], zr[:SROWS], rgn_ref[...])
        for h in range(H_RET):
            mix_ref[0, :, h * 128:(h + 1) * 128] = ret[h].astype(BF16)


def even_sample_mixer(z3, cache_k, cache_v, e, page_table, st_in, ret_gn, diff_gn, lam_vecs, lam_init, s_real):
    DB = z3.shape[0]
    n_pages = page_table.shape[1]
    P = 8 if n_pages % 8 == 0 else (2 if n_pages % 2 == 0 else 1)
    n_steps = n_pages // P
    nrow = 2 * H_DIFF * SROWS
    page_specs = [pl.BlockSpec((None, None, PAGE, 512),
                               (lambda b, j, pt, r=r: (e, pt[b * n_pages + j * P + r], 0, 0)))
                  for r in range(P)]
    grid_spec = pltpu.PrefetchScalarGridSpec(
        num_scalar_prefetch=1,
        grid=(DB, n_steps),
        in_specs=[pl.BlockSpec((1, SROWS, 1536), lambda b, j, pt: (b, 0, 0)),
                  pl.BlockSpec((1, SROWS, 512), lambda b, j, pt: (b, 0, 3)),
                  pl.BlockSpec((1, SROWS, 512), lambda b, j, pt: (b, 0, 4)),
                  pl.BlockSpec((1, SROWS, 512), lambda b, j, pt: (b, 0, 5)),
                  pl.BlockSpec((1, 2, 128, 128), lambda b, j, pt: (b, 0, 0, 0)),
                  pl.BlockSpec((H_RET, DV_RET), lambda b, j, pt: (0, 0)),
                  pl.BlockSpec((H_DIFF, DV_DIFF), lambda b, j, pt: (0, 0)),
                  pl.BlockSpec((4, DH_DIFF), lambda b, j, pt: (0, 0))] + page_specs + page_specs,
        out_specs=[pl.BlockSpec((1, SROWS, 1024), lambda b, j, pt: (b, 0, 0)),
                   pl.BlockSpec((1, 2, 128, 128), lambda b, j, pt: (b, 0, 0, 0))],
        scratch_shapes=[pltpu.VMEM((nrow, 512), BF16),
                        pltpu.VMEM((nrow, 1), F32),
                        pltpu.VMEM((nrow, 1), F32),
                        pltpu.VMEM((nrow, 512), F32)],
    )
    return pl.pallas_call(
        functools.partial(_even_sample_body, P=P, n_steps=n_steps, n_pages=n_pages, lam_init=lam_init,
                          s_real=s_real),
        grid_spec=grid_spec,
        out_shape=[jax.ShapeDtypeStruct((DB, SROWS, 1024), BF16),
                   jax.ShapeDtypeStruct((DB, 2, 128, 128), F32)],
        compiler_params=_params(2),
        name="even_sample_mixer",
    )(page_table.reshape(-1), z3, z3, z3, z3, st_in, ret_gn, diff_gn, lam_vecs,
      *([cache_k] * P), *([cache_v] * P))


def _mla_prep_prompt_body(zc_ref, zg_ref, gcq_ref, gckv_ref, wqa_ref, wqb_ref, wuk_ref, e_ref, wuv_ref,
                          cq_t_ref, sq_t_ref, ck_t_ref, sk_t_ref,
                          ckv_ref, kpe_ref, q_ref, k_ref, v_ref):
    zc = zc_ref[...]
    cq = _rms(zc[:, :D_CQ], gcq_ref[...]).astype(BF16)
    ckv = _rms(zc[:, D_CQ:], gckv_ref[...])
    ckv_ref[...] = ckv
    ckvb = ckv.astype(BF16)
    ct = jnp.concatenate([cq_t_ref[...]] * H_MLA, axis=1)
    st = jnp.concatenate([sq_t_ref[...]] * H_MLA, axis=1)
    q = (_dot(cq, wqa_ref[...]) * ct + _dot(cq, wqb_ref[...]) * st) * MLA_SCALE
    q_ref[...] = q.astype(BF16)
    zg = zg_ref[...]
    kpe = zg[:, G_KPA:G_KPA + D_ROPE] * ck_t_ref[...] + zg[:, G_KPB:G_KPB + D_ROPE] * sk_t_ref[...]
    kpe_ref[...] = kpe
    k_ref[...] = (_dot(ckvb, wuk_ref[...]) + _dot(kpe.astype(BF16), e_ref[...])).astype(BF16)
    v_ref[...] = _dot(ckvb, wuv_ref[...]).astype(BF16)


def mla_prep_prompt(z, w, tabs, tm, tiles_per_seq):
    M = z.shape[0]
    full = lambda a: pl.BlockSpec(a.shape, lambda i: (0,) * a.ndim)
    tab = lambda a: pl.BlockSpec((tm, a.shape[1]), lambda i: (i % tiles_per_seq, 0))
    weights = [w["g_cq"], w["g_ckv"], w["wqa"], w["wqb"], w["wuk_pad"], w["e_place"], w["wuv"]]
    tables = [tabs["cq"], tabs["sq"], tabs["ck"], tabs["sk"]]
    return pl.pallas_call(
        _mla_prep_prompt_body,
        grid=(M // tm,),
        in_specs=[pl.BlockSpec((tm, 512), lambda i: (i, 0)),
                  pl.BlockSpec((tm, LANES), lambda i: (i, O_GATE // LANES))]
                 + [full(a) for a in weights] + [tab(a) for a in tables],
        out_specs=[pl.BlockSpec((tm, D_CKV), lambda i: (i, 0)),
                   pl.BlockSpec((tm, D_ROPE), lambda i: (i, 0)),
                   pl.BlockSpec((tm, H_MLA * 128), lambda i: (i, 0)),
                   pl.BlockSpec((tm, H_MLA * 128), lambda i: (i, 0)),
                   pl.BlockSpec((tm, H_MLA * DV_MLA), lambda i: (i, 0))],
        out_shape=[jax.ShapeDtypeStruct((M, D_CKV), F32),
                   jax.ShapeDtypeStruct((M, D_ROPE), F32),
                   jax.ShapeDtypeStruct((M, H_MLA * 128), BF16),
                   jax.ShapeDtypeStruct((M, H_MLA * 128), BF16),
                   jax.ShapeDtypeStruct((M, H_MLA * DV_MLA), BF16)],
        compiler_params=_params(1),
        name="mla_prep_prompt",
    )(z, z, *weights, *tables)


def _mla_prep_sample_body(zc_ref, zg_ref, gcq_ref, gckv_ref, wqn_ref, wpa_ref, wpb_ref, wukt_ref,
                          cp_t_ref, sp_t_ref, ck_t_ref, sk_t_ref,
                          ckv_ref, kpe_ref, qlat_ref, qpe_ref):
    zc = zc_ref[...]
    cq = _rms(zc[:, :D_CQ], gcq_ref[...]).astype(BF16)
    ckv_ref[...] = _rms(zc[:, D_CQ:], gckv_ref[...])
    zg = zg_ref[...]
    kpe_ref[...] = zg[:, G_KPA:G_KPA + D_ROPE] * ck_t_ref[...] + zg[:, G_KPB:G_KPB + D_ROPE] * sk_t_ref[...]
    qn = _dot(cq, wqn_ref[...]) * MLA_SCALE
    col = lax.broadcasted_iota(jnp.int32, (1, H_MLA * D_NOPE), 1)
    for h in range(H_MLA):
        qh = jnp.where(jnp.right_shift(col, 6) == h, qn, 0.0).astype(BF16)
        qlat_ref[:, h * D_CKV:(h + 1) * D_CKV] = _dot(qh, wukt_ref[...]).astype(BF16)
    qpe = (_dot(cq, wpa_ref[...]) * cp_t_ref[...] + _dot(cq, wpb_ref[...]) * sp_t_ref[...]) * MLA_SCALE
    qpe_ref[...] = qpe.astype(BF16)


def mla_prep_sample(z, w, tabs):
    M = z.shape[0]
    full = lambda a: pl.BlockSpec(a.shape, lambda i: (0,) * a.ndim)
    weights = [w["g_cq"], w["g_ckv"], w["wq_nope"], w["wpe_a"], w["wpe_b"], w["wuk_t"]]
    tables = [tabs["cp"], tabs["sp"], tabs["ck"], tabs["sk"]]
    return pl.pallas_call(
        _mla_prep_sample_body,
        grid=(1,),
        in_specs=[pl.BlockSpec((M, 512), lambda i: (0, 0)),
                  pl.BlockSpec((M, LANES), lambda i: (0, O_GATE // LANES))]
                 + [full(a) for a in weights] + [full(a) for a in tables],
        out_specs=[pl.BlockSpec((M, D_CKV), lambda i: (0, 0)),
                   pl.BlockSpec((M, D_ROPE), lambda i: (0, 0)),
                   pl.BlockSpec((M, H_MLA * D_CKV), lambda i: (0, 0)),
                   pl.BlockSpec((M, H_MLA * D_ROPE), lambda i: (0, 0))],
        out_shape=[jax.ShapeDtypeStruct((M, D_CKV), F32),
                   jax.ShapeDtypeStruct((M, D_ROPE), F32),
                   jax.ShapeDtypeStruct((M, H_MLA * D_CKV), BF16),
                   jax.ShapeDtypeStruct((M, H_MLA * D_ROPE), BF16)],
        compiler_params=_params(1),
        name="mla_prep_sample",
    )(z, z, *weights, *tables)


def _store_mlstm_state(c_ref, n_ref, m_ref, cexts, ms):
    lane = lax.broadcasted_iota(jnp.int32, (1, LANES), 1)
    mrow = jnp.zeros((1, LANES), F32)
    for h in range(H_MLSTM):
        c_ref[0, h] = cexts[h][:, :DV_MLSTM]
        n_ref[0, h:h + 1, :] = cexts[h][:, DV_MLSTM:].T[0:1, :]
        mrow = jnp.where(lane == h, ms[h], mrow)
    m_ref[0] = mrow


def _odd_prompt_body(q_ref, k_ref, v_ref, mq_ref, mk_ref, mv_ref, mo_ref, gt_ref, bvec_ref, gn_ref,
                     mix_ref, c_ref, n_ref, m_ref, cext_ref, ms_ref, *, pad, nc):
    i = pl.program_id(1)

    @pl.when(i == 0)
    def _():
        cext_ref[...] = jnp.zeros_like(cext_ref)
        ms_ref[...] = jnp.zeros_like(ms_ref)

    lane = lax.broadcasted_iota(jnp.int32, (1, LANES), 1)
    lo = lane < DV_MLA
    ti = lax.broadcasted_iota(jnp.int32, (CHUNK, CHUNK), 0)
    si = lax.broadcasted_iota(jnp.int32, (CHUNK, CHUNK), 1)
    for p in range(H_MLA // 2):
        sa = slice(2 * p * 128, (2 * p + 1) * 128)
        sb = slice((2 * p + 1) * 128, (2 * p + 2) * 128)
        sv = slice(p * 128, (p + 1) * 128)
        qa = q_ref[0, :, sa]
        qb = q_ref[0, :, sb]

        def body(j, carry, sa=sa, sb=sb, sv=sv, qa=qa, qb=qb):
            ma, la, mb, lb, acc = carry
            r0 = pl.multiple_of(j * CHUNK, CHUNK)
            ka = k_ref[0, pl.ds(r0, CHUNK), sa]
            kb = k_ref[0, pl.ds(r0, CHUNK), sb]
            vp = v_ref[0, pl.ds(r0, CHUNK), sv]
            kidx = si + j * CHUNK
            valid = jnp.logical_and(kidx <= ti + i * CHUNK, kidx >= pad)
            s_a = jnp.where(valid, _dot_nt(qa, ka), NEG)
            s_b = jnp.where(valid, _dot_nt(qb, kb), NEG)
            ma, la, ca, pa = _softmax_step(ma, la, s_a)
            mb, lb, cb, pb = _softmax_step(mb, lb, s_b)
            va = jnp.where(lo, vp, jnp.zeros_like(vp))
            vb = jnp.where(lo, jnp.zeros_like(vp), vp)
            pv = _dot(pa.astype(BF16), va) + _dot(pb.astype(BF16), vb)
            acc = acc * jnp.where(lo, ca, cb) + pv
            return ma, la, mb, lb, acc

        mi = jnp.full((CHUNK, 1), NEG, F32)
        zl = jnp.zeros((CHUNK, 1), F32)
        ma, la, mb, lb, acc = lax.fori_loop(0, i + 1, body, (mi, zl, mi, zl, jnp.zeros((CHUNK, LANES), F32)))
        mix_ref[0, :, sv] = (acc / jnp.where(lo, la, lb)).astype(BF16)

    row = lax.broadcasted_iota(jnp.int32, (CHUNK, 1), 0)
    valid = row + i * CHUNK >= pad
    cexts = [cext_ref[h] for h in range(H_MLSTM)]
    ms = [ms_ref[h:h + 1, 0:1] for h in range(H_MLSTM)]
    outs, ncs, nms = _mlstm_chunk(mq_ref[0], mk_ref[0], mv_ref[0], mo_ref[0], gt_ref[0], bvec_ref[...], valid,
                                  cexts, ms, gn_ref[...])
    for h in range(H_MLSTM):
        mix_ref[0, :, 512 + h * 128:512 + (h + 1) * 128] = outs[h].astype(BF16)
        cext_ref[h] = ncs[h]
        ms_ref[h:h + 1, :] = jnp.broadcast_to(nms[h], (1, LANES))

    @pl.when(i == nc - 1)
    def _():
        _store_mlstm_state(c_ref, n_ref, m_ref, ncs, nms)


def odd_prompt_mixer(q3, k3, v3, z3, bvec, gn, pad):
    B, Lp, _ = z3.shape
    nc = Lp // CHUNK
    zspec = lambda blk: pl.BlockSpec((1, CHUNK, 512), lambda b, i: (b, i, blk))
    return pl.pallas_call(
        functools.partial(_odd_prompt_body, pad=pad, nc=nc),
        grid=(B, nc),
        in_specs=[pl.BlockSpec((1, CHUNK, H_MLA * 128), lambda b, i: (b, i, 0)),
                  pl.BlockSpec((1, Lp, H_MLA * 128), lambda b, i: (b, 0, 0)),
                  pl.BlockSpec((1, Lp, H_MLA * DV_MLA), lambda b, i: (b, 0, 0)),
                  zspec(O_MQ // 512), zspec(O_MK // 512), zspec(O_MV // 512), zspec(O_MO // 512),
                  pl.BlockSpec((1, CHUNK, LANES), lambda b, i: (b, i, O_GATE // LANES)),
                  pl.BlockSpec((1, LANES), lambda b, i: (0, 0)),
                  pl.BlockSpec((H_MLSTM, DV_MLSTM), lambda b, i: (0, 0))],
        out_specs=[pl.BlockSpec((1, CHUNK, 1024), lambda b, i: (b, i, 0)),
                   pl.BlockSpec((1, H_MLSTM, 128, 128), lambda b, i: (b, 0, 0, 0)),
                   pl.BlockSpec((1, H_MLSTM, 128), lambda b, i: (b, 0, 0)),
                   pl.BlockSpec((1, 1, LANES), lambda b, i: (b, 0, 0))],
        out_shape=[jax.ShapeDtypeStruct((B, Lp, 1024), BF16),
                   jax.ShapeDtypeStruct((B, H_MLSTM, 128, 128), F32),
                   jax.ShapeDtypeStruct((B, H_MLSTM, 128), F32),
                   jax.ShapeDtypeStruct((B, 1, LANES), F32)],
        scratch_shapes=[pltpu.VMEM((H_MLSTM, 128, 256), F32),
                        pltpu.VMEM((8, LANES), F32)],
        compiler_params=_params(2),
        name="odd_prompt_mixer",
    )(q3, k3, v3, z3, z3, z3, z3, z3, bvec, gn)


def _odd_sample_body(pt_ref, qlat_ref, qpe_ref, ckvn_ref, kpen_ref, mq_ref, mk_ref, mv_ref, mo_ref, gt_ref,
                     cin_ref, nin_ref, min_ref, bvec_ref, gn_ref, wuv_ref, *rest,
                     P, n_steps, s_real):
    cp_refs = rest[:P]
    kp_refs = rest[P:2 * P]
    mix_ref, c_ref, n_ref, m_ref, m_sc, l_sc, acc_sc = rest[2 * P:]
    j = pl.program_id(1)
    nrow = H_MLA * SROWS
    row = lax.broadcasted_iota(jnp.int32, (nrow, 1), 0)
    lane = lax.broadcasted_iota(jnp.int32, (1, LANES), 1)
    rr = row & (SROWS - 1)

    @pl.when(j == 0)
    def _():
        m_sc[...] = jnp.full_like(m_sc, NEG)
        l_sc[...] = jnp.zeros_like(l_sc)
        acc_sc[...] = jnp.zeros_like(acc_sc)

    def attend(cb, kb, s_fix):
        s = s_fix(_dot_nt(qlat_ref[0], cb) + _dot_nt(qpe_ref[0], kb))
        m_new, l_new, corr, p = _softmax_step(m_sc[...], l_sc[...], s)
        m_sc[...] = m_new
        l_sc[...] = l_new
        acc_sc[...] = acc_sc[...] * corr + _dot(p.astype(BF16), cb)

    for r in range(P):
        attend(cp_refs[r][...].astype(BF16), kp_refs[r][...].astype(BF16), lambda s: s)

    @pl.when(j == n_steps - 1)
    def _():
        cn = _pad_rows(ckvn_ref[0], PAGE).astype(BF16)
        kn = _pad_rows(kpen_ref[0], PAGE).astype(BF16)

        def new(s):
            return jnp.where(jnp.logical_and(lane >= SROWS - s_real, lane <= rr), s, NEG)

        attend(cn, kn, new)
        o_lat = (acc_sc[...] / l_sc[...]).astype(BF16)
        r_all = _dot(o_lat, wuv_ref[...])
        col = lax.broadcasted_iota(jnp.int32, (1, H_MLA * DV_MLA), 1)
        mla = jnp.zeros((SROWS, H_MLA * DV_MLA), F32)
        for h in range(H_MLA):
            mla = jnp.where(jnp.right_shift(col, 6) == h, r_all[h * SROWS:(h + 1) * SROWS], mla)
        mix_ref[0, :, 0:512] = mla.astype(BF16)

        rowc = lax.broadcasted_iota(jnp.int32, (CHUNK, 1), 0)
        valid = jnp.logical_and(rowc >= SROWS - s_real, rowc < SROWS)
        cexts, ms = [], []
        for h in range(H_MLSTM):
            ncol = jnp.broadcast_to(nin_ref[0, h:h + 1, :], (LANES, DK_MLSTM)).T
            cexts.append(jnp.concatenate([cin_ref[0, h], ncol], axis=1))
            ms.append(min_ref[0, 0:1, h:h + 1])
        outs, ncs, nms = _mlstm_chunk(_pad_rows(mq_ref[0], CHUNK), _pad_rows(mk_ref[0], CHUNK),
                                      _pad_rows(mv_ref[0], CHUNK), _pad_rows(mo_ref[0], CHUNK),
                                      _pad_rows(gt_ref[0], CHUNK), bvec_ref[...], valid, cexts, ms, gn_ref[...])
        for h in range(H_MLSTM):
            mix_ref[0, :, 512 + h * 128:512 + (h + 1) * 128] = outs[h][:SROWS].astype(BF16)
        _store_mlstm_state(c_ref, n_ref, m_ref, ncs, nms)


def odd_sample_mixer(qlat3, qpe3, ckvn3, kpen3, z3, cache_ckv, cache_kpe, od, page_table, c_in, n_in, m_in,
                     bvec, gn, wuv, s_real):
    DB = z3.shape[0]
    n_pages = page_table.shape[1]
    P = 8 if n_pages % 8 == 0 else (2 if n_pages % 2 == 0 else 1)
    n_steps = n_pages // P
    nrow = H_MLA * SROWS
    pidx = lambda b, j, pt, r: (od, pt[b * n_pages + j * P + r], 0, 0)
    ckv_specs = [pl.BlockSpec((None, None, PAGE, D_CKV), functools.partial(pidx, r=r)) for r in range(P)]
    kpe_specs = [pl.BlockSpec((None, None, PAGE, D_ROPE), functools.partial(pidx, r=r)) for r in range(P)]
    zspec = lambda blk: pl.BlockSpec((1, SROWS, 512), lambda b, j, pt: (b, 0, blk))
    grid_spec = pltpu.PrefetchScalarGridSpec(
        num_scalar_prefetch=1,
        grid=(DB, n_steps),
        in_specs=[pl.BlockSpec((1, nrow, D_CKV), lambda b, j, pt: (b, 0, 0)),
                  pl.BlockSpec((1, nrow, D_ROPE), lambda b, j, pt: (b, 0, 0)),
                  pl.BlockSpec((1, SROWS, D_CKV), lambda b, j, pt: (b, 0, 0)),
                  pl.BlockSpec((1, SROWS, D_ROPE), lambda b, j, pt: (b, 0, 0)),
                  zspec(O_MQ // 512), zspec(O_MK // 512), zspec(O_MV // 512), zspec(O_MO // 512),
                  pl.BlockSpec((1, SROWS, LANES), lambda b, j, pt: (b, 0, O_GATE // LANES)),
                  pl.BlockSpec((1, H_MLSTM, 128, 128), lambda b, j, pt: (b, 0, 0, 0)),
                  pl.BlockSpec((1, H_MLSTM, 128), lambda b, j, pt: (b, 0, 0)),
                  pl.BlockSpec((1, 1, LANES), lambda b, j, pt: (b, 0, 0)),
                  pl.BlockSpec((1, LANES), lambda b, j, pt: (0, 0)),
                  pl.BlockSpec((H_MLSTM, DV_MLSTM), lambda b, j, pt: (0, 0)),
                  pl.BlockSpec((D_CKV, H_MLA * DV_MLA), lambda b, j, pt: (0, 0))] + ckv_specs + kpe_specs,
        out_specs=[pl.BlockSpec((1, SROWS, 1024), lambda b, j, pt: (b, 0, 0)),
                   pl.BlockSpec((1, H_MLSTM, 128, 128), lambda b, j, pt: (b, 0, 0, 0)),
                   pl.BlockSpec((1, H_MLSTM, 128), lambda b, j, pt: (b, 0, 0)),
                   pl.BlockSpec((1, 1, LANES), lambda b, j, pt: (b, 0, 0))],
        scratch_shapes=[pltpu.VMEM((nrow, 1), F32),
                        pltpu.VMEM((nrow, 1), F32),
                        pltpu.VMEM((nrow, D_CKV), F32)],
    )
    return pl.pallas_call(
        functools.partial(_odd_sample_body, P=P, n_steps=n_steps, s_real=s_real),
        grid_spec=grid_spec,
        out_shape=[jax.ShapeDtypeStruct((DB, SROWS, 1024), BF16),
                   jax.ShapeDtypeStruct((DB, H_MLSTM, 128, 128), F32),
                   jax.ShapeDtypeStruct((DB, H_MLSTM, 128), F32),
                   jax.ShapeDtypeStruct((DB, 1, LANES), F32)],
        compiler_params=_params(2),
        name="odd_sample_mixer",
    )(page_table.reshape(-1), qlat3, qpe3, ckvn3, kpen3, z3, z3, z3, z3, z3, c_in, n_in, m_in, bvec, gn, wuv,
      *([cache_ckv] * P), *([cache_kpe] * P))


def _odd_w_in(w):
    D = w.shape[0]
    kp1, kp2 = w[:, 512:528], w[:, 528:544]
    gate = jnp.concatenate([kp1, kp2, kp2, kp1, w[:, 2592:2596], w[:, 2596:2600],
                            jnp.zeros((D, LANES - 72), w.dtype)], axis=1)
    return jnp.concatenate([w[:, 0:512], w[:, 544:2592], gate], axis=1).astype(BF16)


def _mla_weights(g_cq, g_ckv, w_uq, w_uk, w_uv):
    C = w_uq.shape[0]
    nope, pe1, pe2 = w_uq[:, :, :D_NOPE], w_uq[:, :, D_NOPE:D_NOPE + 16], w_uq[:, :, D_NOPE + 16:]
    z64 = jnp.zeros((C, H_MLA, D_NOPE), w_uq.dtype)
    z32 = jnp.zeros((C, H_MLA, 32), w_uq.dtype)
    wqa = jnp.concatenate([nope, pe1, pe2, z32], axis=2).reshape(C, H_MLA * 128)
    wqb = jnp.concatenate([z64, pe2, pe1, z32], axis=2).reshape(C, H_MLA * 128)
    wuk_pad = jnp.concatenate([w_uk, jnp.zeros((C, H_MLA, 64), w_uk.dtype)], axis=2).reshape(C, H_MLA * 128)
    e1 = jnp.concatenate([jnp.zeros((D_ROPE, D_NOPE), F32), jnp.eye(D_ROPE, dtype=F32),
                          jnp.zeros((D_ROPE, 32), F32)], axis=1)
    return {
        "g_cq": g_cq.reshape(1, -1), "g_ckv": g_ckv.reshape(1, -1),
        "wqa": wqa.astype(BF16), "wqb": wqb.astype(BF16), "wuk_pad": wuk_pad.astype(BF16),
        "e_place": jnp.tile(e1, (1, H_MLA)).astype(BF16),
        "wuv": w_uv.reshape(C, H_MLA * DV_MLA).astype(BF16),
        "wq_nope": nope.reshape(C, H_MLA * D_NOPE).astype(BF16),
        "wpe_a": jnp.concatenate([pe1, pe2], axis=2).reshape(C, H_MLA * D_ROPE).astype(BF16),
        "wpe_b": jnp.concatenate([pe2, pe1], axis=2).reshape(C, H_MLA * D_ROPE).astype(BF16),
        "wuk_t": w_uk.transpose(1, 2, 0).reshape(H_MLA * D_NOPE, C).astype(BF16),
    }


def _rope_tables(pos):
    half = D_ROPE // 2
    freq = ROPE_THETA ** (-jnp.arange(half, dtype=F32) / half)
    ang = pos.astype(F32)[:, None] * freq[None, :]
    cos, sin = jnp.cos(ang), jnp.sin(ang)
    n = pos.shape[0]
    ck = jnp.concatenate([cos, cos], axis=1)
    sk = jnp.concatenate([-sin, sin], axis=1)
    cq = jnp.concatenate([jnp.ones((n, D_NOPE), F32), ck, jnp.zeros((n, 32), F32)], axis=1)
    sq = jnp.concatenate([jnp.zeros((n, D_NOPE), F32), sk, jnp.zeros((n, 32), F32)], axis=1)
    return {"ck": ck, "sk": sk, "cq": cq, "sq": sq,
            "cp": jnp.tile(ck, (1, H_MLA)), "sp": jnp.tile(sk, (1, H_MLA))}


def _lambda_init(layer):
    return 0.8 - 0.6 * math.exp(-0.3 * layer)


def kernel(x_prompt, x_sample, cache_diff_k, cache_diff_v, cache_mla_ckv, cache_mla_kpe, page_table, state_ret, state_mlstm_c, state_mlstm_n, state_mlstm_m, state_ffn_conv, meta_tokens, norm_mix, norm_ffn, norm_final, w_in_even, w_out_even, ret_gn, diff_gn, lam_q1, lam_k1, lam_q2, lam_k2, w_in_odd, w_out_odd, mla_g_cq, mla_g_ckv, mla_w_uq, mla_w_uk, mla_w_uv, mlstm_b_i, mlstm_b_f, mlstm_gn, ffn_w_gate, ffn_w_up, ffn_conv_w, ffn_conv_b, ffn_w_down):
    B, SEQ, D = x_prompt.shape
    DB, S, _ = x_sample.shape
    depth = norm_mix.shape[0]
    F = ffn_w_gate.shape[2]
    n_pages = page_table.shape[1]
    assert S <= SROWS
    L = SEQ + N_META
    pad = (-L) % CHUNK
    Lp = L + pad
    spad = SROWS - S
    MP, MS = B * Lp, DB * SROWS
    tmp = Lp // 2 if (Lp // 2) % 16 == 0 else Lp
    tps = Lp // tmp
    rc = Lp // 4 if Lp % 64 == 0 and Lp >= 1024 else Lp
    tf = 256

    xp = jnp.concatenate([jnp.zeros((B, pad, D), F32),
                          jnp.broadcast_to(meta_tokens[None], (B, N_META, D)), x_prompt], axis=1).reshape(MP, D)
    xs = jnp.concatenate([jnp.zeros((DB, spad, D), F32), x_sample], axis=1).reshape(MS, D)

    ck_pages = cache_diff_k.reshape(cache_diff_k.shape[0], cache_diff_k.shape[1], PAGE, -1)
    cv_pages = cache_diff_v.reshape(cache_diff_v.shape[0], cache_diff_v.shape[1], PAGE, -1)

    pos_p = jnp.maximum(jnp.arange(Lp) - pad, 0)
    pos_s = jnp.tile(jnp.maximum(n_pages * PAGE + jnp.arange(SROWS) - spad, 0), DB)
    tabs_p = _rope_tables(pos_p)
    tabs_s = _rope_tables(pos_s)

    dk_p, dv_p, dk_s, dv_s, ret_p, ret_s = [], [], [], [], [], []
    ckv_p, kpe_p, ckv_s, kpe_s = [], [], [], []
    mc_p, mn_p, mm_p, mc_s, mn_s, mm_s = [], [], [], [], [], []
    cb_p, cb_s = [], []
    for layer in range(depth):
        if layer % 2 == 0:
            e = layer // 2
            li = _lambda_init(layer)
            w_in = w_in_even[e].astype(BF16)
            w_out = w_out_even[e].astype(BF16)
            lam_vecs = jnp.stack([lam_q1[e], lam_k1[e], lam_q2[e], lam_k2[e]])
            zp = norm_matmul(xp, norm_mix[layer], w_in, tmp, 512).reshape(B, Lp, -1)
            zs = norm_matmul(xs, norm_mix[layer], w_in, MS, 512).reshape(DB, SROWS, -1)
            mix_p, st_p = even_prompt_mixer(zp, ret_gn[e], diff_gn[e], lam_vecs, li, pad)
            mix_s, st_s = even_sample_mixer(zs, ck_pages, cv_pages, e, page_table,
                                            state_ret[e].reshape(DB, 2, 128, 128),
                                            ret_gn[e], diff_gn[e], lam_vecs, li, S)
            dk_p.append(zp[:, pad:, 2048:2560].reshape(B, L, H_DIFF, 2 * DH_DIFF))
            dv_p.append(zp[:, pad:, 2560:3072].reshape(B, L, H_DIFF, DV_DIFF))
            dk_s.append(zs[:, spad:, 2048:2560].reshape(DB, S, H_DIFF, 2 * DH_DIFF))
            dv_s.append(zs[:, spad:, 2560:3072].reshape(DB, S, H_DIFF, DV_DIFF))
            ret_p.append(st_p.reshape(B, H_RET, DK_RET, DV_RET))
            ret_s.append(st_s.reshape(DB, H_RET, DK_RET, DV_RET))
        else:
            od = layer // 2
            w_in = _odd_w_in(w_in_odd[od])
            w_out = w_out_odd[od].astype(BF16)
            mw = _mla_weights(mla_g_cq[od], mla_g_ckv[od], mla_w_uq[od], mla_w_uk[od], mla_w_uv[od])
            bvec = jnp.concatenate([jnp.zeros((G_MI,), F32), mlstm_b_i[od], mlstm_b_f[od],
                                    jnp.zeros((LANES - G_MF - H_MLSTM,), F32)]).reshape(1, LANES)
            zp2 = norm_matmul(xp, norm_mix[layer], w_in, tmp, 384)
            zs2 = norm_matmul(xs, norm_mix[layer], w_in, MS, 384)
            ckv_n, kpe_r, q_b, k_b, v_b = mla_prep_prompt(zp2, mw, tabs_p, tmp, tps)
            mix_p, c_p, n_p, m_p = odd_prompt_mixer(
                q_b.reshape(B, Lp, -1), k_b.reshape(B, Lp, -1), v_b.reshape(B, Lp, -1),
                zp2.reshape(B, Lp, -1), bvec, mlstm_gn[od], pad)
            ckv_ns, kpe_rs, qlat, qpe = mla_prep_sample(zs2, mw, tabs_s)
            qlat3 = qlat.reshape(DB, SROWS, H_MLA, D_CKV).transpose(0, 2, 1, 3).reshape(DB, H_MLA * SROWS, D_CKV)
            qpe3 = qpe.reshape(DB, SROWS, H_MLA, D_ROPE).transpose(0, 2, 1, 3).reshape(DB, H_MLA * SROWS, D_ROPE)
            m_in = jnp.pad(state_mlstm_m[od], ((0, 0), (0, LANES - H_MLSTM))).reshape(DB, 1, LANES)
            mix_s, c_s, n_s, m_s = odd_sample_mixer(
                qlat3, qpe3, ckv_ns.reshape(DB, SROWS, -1), kpe_rs.reshape(DB, SROWS, -1),
                zs2.reshape(DB, SROWS, -1), cache_mla_ckv, cache_mla_kpe, od, page_table,
                state_mlstm_c[od], state_mlstm_n[od], m_in, bvec, mlstm_gn[od], mw["wuv"], S)
            ckv_p.append(ckv_n.reshape(B, Lp, -1)[:, pad:])
            kpe_p.append(kpe_r.reshape(B, Lp, -1)[:, pad:])
            ckv_s.append(ckv_ns.reshape(DB, SROWS, -1)[:, spad:])
            kpe_s.append(kpe_rs.reshape(DB, SROWS, -1)[:, spad:])
            mc_p.append(c_p)
            mn_p.append(n_p)
            mm_p.append(m_p[:, 0, :H_MLSTM])
            mc_s.append(c_s)
            mn_s.append(n_s)
            mm_s.append(m_s[:, 0, :H_MLSTM])
        xp = proj_residual(xp, mix_p.reshape(MP, -1), w_out, tmp, Lp, pad)
        xs = proj_residual(xs, mix_s.reshape(MS, -1), w_out, MS, SROWS, spad)
        wg, wu, wd = ffn_w_gate[layer].astype(BF16), ffn_w_up[layer].astype(BF16), ffn_w_down[layer].astype(BF16)
        gstate = jnp.pad(state_ffn_conv[layer], ((0, 0), (SROWS - S - (CONV_W - 1), S), (0, 0))).reshape(MS, F)
        gstate = jnp.pad(gstate, ((HALO, 0), (0, 0)))
        xp, conv_p = conv_ffn(xp, norm_ffn[layer], wg, wu, ffn_conv_w[layer], ffn_conv_b[layer], wd, Lp, rc, tf)
        xs, conv_s = conv_ffn(xs, norm_ffn[layer], wg, wu, ffn_conv_w[layer], ffn_conv_b[layer], wd, MS, MS, tf,
                              gstate=gstate)
        cb_p.append(conv_p)
        cb_s.append(conv_s.reshape(DB, SROWS, F)[:, SROWS - (CONV_W - 1):])

    off = pad + N_META
    xp3 = xp.reshape(B, Lp, D)
    if off % CHUNK == 0 and SEQ % CHUNK == 0:
        y_prompt = final_norm(xp3, norm_final, off, SEQ, CHUNK)
    else:
        y_prompt = final_norm(xp3, norm_final, 0, Lp, CHUNK)[:, off:]
    y_sample = final_norm(xs.reshape(DB, SROWS, D), norm_final, 0, SROWS, SROWS)[:, spad:]

    st = jnp.stack
    return (y_prompt, y_sample,
            st(dk_p), st(dv_p), st(dk_s), st(dv_s),
            st(ckv_p), st(kpe_p), st(ckv_s), st(kpe_s),
            st(ret_p), st(ret_s),
            st(mc_p), st(mn_p), st(mm_p), st(mc_s), st(mn_s), st(mm_s),
            st(cb_p), st(cb_s))
```

```python
import functools
import math

import jax
import jax.numpy as jnp
import numpy as np
from jax import lax
from jax.experimental import pallas as pl
from jax.experimental.pallas import tpu as pltpu

F32 = jnp.float32
BF16 = jnp.bfloat16

N_META = 16
CHUNK = 128
NORM_EPS = 1e-6
NEG = -1e30
H_RET, DK_RET, DV_RET = 4, 64, 128
H_DIFF, DH_DIFF, DV_DIFF = 4, 64, 128
H_MLA, D_NOPE, D_ROPE, DV_MLA = 8, 64, 32, 64
D_CQ, D_CKV = 256, 256
ROPE_THETA = 10000.0
H_MLSTM, DK_MLSTM, DV_MLSTM = 4, 128, 128
CONV_W = 3
PAGE = 128
SROWS = 8
LANES = 128
HALO = 16
KV_BLOCK = 512
VMEM_LIMIT = 56 * 1024 * 1024

O_CQ, O_CKV, O_MQ, O_MK, O_MV, O_MO, O_GATE = 0, 256, 512, 1024, 1536, 2048, 2560
O_IN_PAD = 2688
G_KPA, G_KPB, G_MI, G_MF = 0, 32, 64, 68

RET_LOG_GAMMA = [float(np.log(np.float32(1.0) - np.float32(2.0) ** np.float32(-5.0 - h))) for h in range(H_RET)]
ALIBI = [float(2.0 ** (-8.0 * (h + 1) / H_DIFF)) for h in range(H_DIFF)]
MLA_SCALE = float((D_NOPE + D_ROPE) ** -0.5)


def _params(n_axes, vmem=None):
    return pltpu.CompilerParams(dimension_semantics=("arbitrary",) * n_axes,
                                vmem_limit_bytes=vmem or VMEM_LIMIT)


def _dot(a, b):
    return jnp.dot(a, b, preferred_element_type=F32)


def _dot_nt(a, b):
    return lax.dot_general(a, b, (((1,), (1,)), ((), ())), preferred_element_type=F32)


def _dot_tn(a, b):
    return lax.dot_general(a, b, (((0,), (0,)), ((), ())), preferred_element_type=F32)


def _rms(x, g):
    return x * lax.rsqrt(jnp.mean(x * x, axis=-1, keepdims=True) + NORM_EPS) * g


def _layernorm(x, g):
    xc = x - jnp.mean(x, axis=-1, keepdims=True)
    return xc * lax.rsqrt(jnp.mean(xc * xc, axis=-1, keepdims=True) + NORM_EPS) * g


def _sigmoid(x):
    return 1.0 / (1.0 + jnp.exp(-x))


def _log_sigmoid(x):
    return -(jnp.maximum(-x, 0.0) + jnp.log1p(jnp.exp(-jnp.abs(x))))


def _row_valid(tm, seq_len, pad, i):
    r = lax.broadcasted_iota(jnp.int32, (tm, 1), 0)
    if tm % seq_len == 0 and seq_len & (seq_len - 1) == 0:
        pos = r & (seq_len - 1)
    else:
        assert seq_len % tm == 0
        pos = r + lax.rem(i * tm, seq_len)
    return pos >= pad


def _norm_matmul_body(x_ref, g_ref, w_ref, o_ref, h_ref):
    @pl.when(pl.program_id(1) == 0)
    def _():
        h_ref[...] = _rms(x_ref[...], g_ref[...]).astype(BF16)

    o_ref[...] = _dot(h_ref[...], w_ref[...])


def norm_matmul(x, g, w, tm, tn):
    M, K = x.shape
    N = w.shape[1]
    return pl.pallas_call(
        _norm_matmul_body,
        grid=(M // tm, N // tn),
        in_specs=[pl.BlockSpec((tm, K), lambda i, j: (i, 0)),
                  pl.BlockSpec((1, K), lambda i, j: (0, 0)),
                  pl.BlockSpec((K, tn), lambda i, j: (0, j))],
        out_specs=pl.BlockSpec((tm, tn), lambda i, j: (i, j)),
        out_shape=jax.ShapeDtypeStruct((M, N), F32),
        scratch_shapes=[pltpu.VMEM((tm, K), BF16)],
        compiler_params=_params(2),
        name="norm_matmul",
    )(x, g.reshape(1, K), w)


def _proj_res_body(x_ref, a_ref, w_ref, o_ref, *, tm, seq_len, pad):
    y = x_ref[...] + _dot(a_ref[...], w_ref[...])
    o_ref[...] = jnp.where(_row_valid(tm, seq_len, pad, pl.program_id(0)), y, 0.0)


def proj_residual(x, a, w, tm, seq_len, pad):
    M, D = x.shape
    E = a.shape[1]
    return pl.pallas_call(
        functools.partial(_proj_res_body, tm=tm, seq_len=seq_len, pad=pad),
        grid=(M // tm,),
        in_specs=[pl.BlockSpec((tm, D), lambda i: (i, 0)),
                  pl.BlockSpec((tm, E), lambda i: (i, 0)),
                  pl.BlockSpec((E, D), lambda i: (0, 0))],
        out_specs=pl.BlockSpec((tm, D), lambda i: (i, 0)),
        out_shape=jax.ShapeDtypeStruct((M, D), F32),
        compiler_params=_params(1),
        name="proj_residual",
    )(x, a, w)


def _ffn_body(*refs, tm, rc, has_state):
    if has_state:
        x_ref, gn_ref, wg_ref, wu_ref, cw_ref, cb_ref, wd_ref, gs_ref, y_ref, conv_ref, h_ref = refs
    else:
        x_ref, gn_ref, wg_ref, wu_ref, cw_ref, cb_ref, wd_ref, y_ref, conv_ref, h_ref = refs
        gs_ref = None
    j = pl.program_id(1)
    n_chunks = tm // rc
    K = x_ref.shape[-1]

    @pl.when(j == 0)
    def _():
        h_ref[pl.ds(0, HALO), :] = jnp.zeros((HALO, K), BF16)

        def norm_chunk(c, carry):
            r0 = pl.multiple_of(c * rc, HALO)
            x = x_ref[pl.ds(r0, rc), :]
            h_ref[pl.ds(r0 + HALO, rc), :] = _rms(x, gn_ref[...]).astype(BF16)
            y_ref[pl.ds(r0, rc), :] = x
            return carry

        lax.fori_loop(0, n_chunks, norm_chunk, 0)

    cw = cw_ref[...]
    cb = cb_ref[...]

    def chunk(c, carry):
        r0 = pl.multiple_of(c * rc, HALO)
        hx = h_ref[pl.ds(r0, rc + HALO), :]
        g = _dot(hx, wg_ref[...])
        if has_state:
            g = g + gs_ref[...]
        u = _dot(hx[HALO:], wu_ref[...])
        g0 = g[HALO:]
        g1 = pltpu.roll(g, 1, 0)[HALO:]
        g2 = pltpu.roll(g, 2, 0)[HALO:]
        gc = cb + cw[0:1] * g2 + cw[1:2] * g1 + cw[2:3] * g0
        a = (gc * _sigmoid(gc)) * u
        y_ref[pl.ds(r0, rc), :] += _dot(a.astype(BF16), wd_ref[...])
        if has_state:
            conv_ref[...] = g0
        return carry

    lax.fori_loop(0, n_chunks, chunk, 0)
    if not has_state:
        tail = _dot(h_ref[pl.ds(tm, HALO), :], wg_ref[...])
        conv_ref[0] = tail[HALO - (CONV_W - 1):]


def conv_ffn(x, gn, wg, wu, cw, cb, wd, tm, rc, tf, gstate=None):
    M, D = x.shape
    F = wg.shape[1]
    has_state = gstate is not None
    if has_state:
        assert rc == tm == M
    in_specs = [pl.BlockSpec((tm, D), lambda i, j: (i, 0)),
                pl.BlockSpec((1, D), lambda i, j: (0, 0)),
                pl.BlockSpec((D, tf), lambda i, j: (0, j)),
                pl.BlockSpec((D, tf), lambda i, j: (0, j)),
                pl.BlockSpec((CONV_W, tf), lambda i, j: (0, j)),
                pl.BlockSpec((1, tf), lambda i, j: (0, j)),
                pl.BlockSpec((tf, D), lambda i, j: (j, 0))]
    args = [x, gn.reshape(1, D), wg, wu, cw, cb.reshape(1, F), wd]
    if has_state:
        in_specs.append(pl.BlockSpec((tm + HALO, tf), lambda i, j: (0, j)))
        args.append(gstate)
        conv_spec = pl.BlockSpec((tm, tf), lambda i, j: (0, j))
        conv_shape = jax.ShapeDtypeStruct((M, F), F32)
    else:
        conv_spec = pl.BlockSpec((1, CONV_W - 1, tf), lambda i, j: (i, 0, j))
        conv_shape = jax.ShapeDtypeStruct((M // tm, CONV_W - 1, F), F32)
    return pl.pallas_call(
        functools.partial(_ffn_body, tm=tm, rc=rc, has_state=has_state),
        grid=(M // tm, F // tf),
        in_specs=in_specs,
        out_specs=[pl.BlockSpec((tm, D), lambda i, j: (i, 0)), conv_spec],
        out_shape=[jax.ShapeDtypeStruct((M, D), F32), conv_shape],
        scratch_shapes=[pltpu.VMEM((tm + HALO, D), BF16)],
        compiler_params=_params(2),
        name="conv_ffn",
    )(*args)


def _final_norm_body(x_ref, g_ref, o_ref):
    o_ref[0] = _rms(x_ref[0], g_ref[...])


def final_norm(x3, g, row_off, rows_out, tr):
    B, L, D = x3.shape
    assert row_off % tr == 0 and rows_out % tr == 0
    ob = row_off // tr
    return pl.pallas_call(
        _final_norm_body,
        grid=(B, rows_out // tr),
        in_specs=[pl.BlockSpec((1, tr, D), lambda b, i: (b, i + ob, 0)),
                  pl.BlockSpec((1, D), lambda b, i: (0, 0))],
        out_specs=pl.BlockSpec((1, tr, D), lambda b, i: (b, i, 0)),
        out_shape=jax.ShapeDtypeStruct((B, rows_out, D), F32),
        compiler_params=_params(2),
        name="final_norm",
    )(x3, g.reshape(1, D))


def _retention_chunk(zr, s_pairs, t_off, t_eff):
    T = zr.shape[0]
    lane = lax.broadcasted_iota(jnp.int32, (1, LANES), 1)
    first = lane < DK_RET
    ti = lax.broadcasted_iota(jnp.int32, (T, T), 0)
    si = lax.broadcasted_iota(jnp.int32, (T, T), 1)
    dist = (ti - si).astype(F32)
    tau = lax.broadcasted_iota(jnp.int32, (T, 1), 0).astype(F32) - float(t_off)
    rowi = lax.broadcasted_iota(jnp.int32, (LANES, 1), 0)
    outs, new_pairs = [], []
    for p in range(H_RET // 2):
        qp = zr[:, p * 128:(p + 1) * 128]
        kp = zr[:, 256 + p * 128:256 + (p + 1) * 128] * (DK_RET ** -0.5)
        kb = kp.astype(BF16)
        sb = s_pairs[p].astype(BF16)
        upd = None
        for s in range(2):
            h = 2 * p + s
            lg = RET_LOG_GAMMA[h]
            msk = first if s == 0 else jnp.logical_not(first)
            qh = jnp.where(msk, qp, 0.0).astype(BF16)
            decay = jnp.where(dist >= 0, jnp.exp(lg * jnp.maximum(dist, 0.0)), 0.0)
            qk = (_dot_nt(qh, kb) * decay).astype(BF16)
            vh = zr[:, 512 + h * 128:512 + (h + 1) * 128].astype(BF16)
            inner = _dot(qk, vh)
            cross = _dot(qh, sb) * jnp.exp(lg * (tau + 1.0))
            outs.append(inner + cross)
            kw = jnp.where(msk, kp * jnp.exp(lg * (float(t_eff) - 1.0 - tau)), 0.0).astype(BF16)
            u = _dot_tn(kw, vh)
            upd = u if upd is None else upd + u
        g0 = float(np.exp(np.float32(RET_LOG_GAMMA[2 * p]) * np.float32(t_eff)))
        g1 = float(np.exp(np.float32(RET_LOG_GAMMA[2 * p + 1]) * np.float32(t_eff)))
        new_pairs.append(jnp.where(rowi < DK_RET, g0, g1) * s_pairs[p] + upd)
    return outs, new_pairs


def _retention_mix(outs, zr, rgn):
    res = []
    for h in range(H_RET):
        rg = zr[:, 1024 + h * 128:1024 + (h + 1) * 128]
        gate = rg * _sigmoid(rg)
        res.append(_layernorm(outs[h], rgn[h:h + 1]) * gate)
    return res


def _softmax_step(m, l, s):
    m_new = jnp.maximum(m, jnp.max(s, axis=-1, keepdims=True))
    corr = jnp.exp(m - m_new)
    p = jnp.exp(s - m_new)
    return m_new, l * corr + jnp.sum(p, axis=-1, keepdims=True), corr, p


def _diff_lambda(lam_ref, lam_init):
    lv = lam_ref[...]
    return (jnp.exp(jnp.sum(lv[0:1] * lv[1:2], axis=-1, keepdims=True))
            - jnp.exp(jnp.sum(lv[2:3] * lv[3:4], axis=-1, keepdims=True)) + lam_init)


def _mlstm_gates(gt, bvec, valid):
    gb = gt + bvec
    li = jnp.where(valid, gb, NEG)
    lf = jnp.where(valid, _log_sigmoid(gb), 0.0)
    row = lax.broadcasted_iota(jnp.int32, (CHUNK, 1), 0)
    sh = 1
    while sh < CHUNK:
        lf = lf + jnp.where(row >= sh, pltpu.roll(lf, sh, 0), 0.0)
        sh *= 2
    lane = lax.broadcasted_iota(jnp.int32, (1, LANES), 1)
    x = jnp.where(lane >= G_MF, lf, li)
    return x, x.T


def _mlstm_head(h, x, xt, q, k, v, cext, m_prev):
    bc = x[:, G_MF + h:G_MF + h + 1]
    br = xt[G_MF + h:G_MF + h + 1, :]
    lic = x[:, G_MI + h:G_MI + h + 1]
    lir = xt[G_MI + h:G_MI + h + 1, :]
    b_last = x[CHUNK - 1:CHUNK, G_MF + h:G_MF + h + 1]
    ti = lax.broadcasted_iota(jnp.int32, (CHUNK, CHUNK), 0)
    si = lax.broadcasted_iota(jnp.int32, (CHUNK, CHUNK), 1)
    d = jnp.where(si <= ti, bc - br + lir, NEG)
    inter = bc + m_prev
    m_q = jnp.maximum(inter, jnp.max(d, axis=-1, keepdims=True))
    w = jnp.exp(d - m_q)
    gq = jnp.exp(inter - m_q)
    qb = q.astype(BF16)
    kb = k.astype(BF16)
    vb = v.astype(BF16)
    qk = _dot_nt(qb, kb) * w
    r = _dot(qb, cext.astype(BF16))
    num = _dot(qk.astype(BF16), vb) + gq * r[:, :DV_MLSTM]
    den = jnp.sum(qk, axis=-1, keepdims=True) + gq * r[:, DV_MLSTM:DV_MLSTM + 1]
    h_out = num / jnp.maximum(jnp.abs(den), jnp.exp(-m_q))
    a = b_last - bc + lic
    m_new = jnp.maximum(b_last + m_prev, jnp.max(a, axis=0, keepdims=True))
    ws = jnp.exp(a - m_new)
    gs = jnp.exp(b_last + m_prev - m_new)
    kw = (k * ws).astype(BF16)
    vext = jnp.concatenate([vb, jnp.ones((CHUNK, LANES), BF16)], axis=1)
    return h_out, gs * cext + _dot_tn(kw, vext), m_new


def _mlstm_chunk(mq, mk, mv, mo, gt, bvec, valid, cexts, ms, gn):
    x, xt = _mlstm_gates(gt, bvec, valid)
    outs, ncs, nms = [], [], []
    for h in range(H_MLSTM):
        sl = slice(h * 128, (h + 1) * 128)
        ho, nc, nm = _mlstm_head(h, x, xt, mq[:, sl], mk[:, sl] * (DK_MLSTM ** -0.5), mv[:, sl],
                                 cexts[h], ms[h])
        outs.append(_layernorm(_sigmoid(mo[:, sl]) * ho, gn[h:h + 1]))
        ncs.append(nc)
        nms.append(nm)
    return outs, ncs, nms


def _pad_rows(a, rows):
    return jnp.concatenate([a, jnp.zeros((rows - a.shape[0], a.shape[1]), a.dtype)], axis=0)


def _causal_block(jb, i, seq_rows, bk, pad):
    base = jb * bk
    r0 = pl.multiple_of(jnp.minimum(base, seq_rows - bk), CHUNK)
    kidx = r0 + lax.broadcasted_iota(jnp.int32, (1, bk), 1)
    qidx = i * CHUNK + lax.broadcasted_iota(jnp.int32, (CHUNK, 1), 0)
    dist = qidx - kidx
    valid = jnp.logical_and(dist >= 0, kidx >= jnp.maximum(base, pad))
    return r0, valid, dist


def _num_kv_blocks(i, bk):
    per = bk // CHUNK
    return lax.div(i + per, per)


def _even_prompt_body(zr_ref, zq_ref, zk_ref, zv_ref, rgn_ref, dgn_ref, lam_ref, mix_ref, st_ref, s_ref,
                      q_sc, m_sc, acc_sc, *, lam_init, pad, nc):
    i = pl.program_id(1)

    @pl.when(i == 0)
    def _():
        s_ref[...] = jnp.zeros_like(s_ref)

    zr = zr_ref[0]
    outs, new_pairs = _retention_chunk(zr, [s_ref[0], s_ref[1]], 0, CHUNK)
    s_ref[0] = new_pairs[0]
    s_ref[1] = new_pairs[1]
    ret = _retention_mix(outs, zr, rgn_ref[...])
    for h in range(H_RET):
        mix_ref[0, :, h * 128:(h + 1) * 128] = ret[h].astype(BF16)

    lam = _diff_lambda(lam_ref, lam_init)
    lane = lax.broadcasted_iota(jnp.int32, (1, LANES), 1)
    first = lane < DH_DIFF
    dgn = dgn_ref[...]
    for g in range(H_DIFF):
        qg = zq_ref[0, :, g * 128:(g + 1) * 128] * (DH_DIFF ** -0.5)
        q_sc[2 * g] = jnp.where(first, qg, 0.0).astype(BF16)
        q_sc[2 * g + 1] = jnp.where(first, 0.0, qg).astype(BF16)
    m_sc[...] = jnp.full_like(m_sc, NEG)
    acc_sc[...] = jnp.zeros_like(acc_sc)
    bk = zk_ref.shape[1] if zk_ref.shape[1] < KV_BLOCK else KV_BLOCK
    ones = jnp.ones((bk, LANES), BF16)

    def body(jb, carry):
        r0, valid, dist = _causal_block(jb, i, zk_ref.shape[1], bk, pad)
        distf = dist.astype(F32)
        for g in range(H_DIFF):
            sl = slice(g * 128, (g + 1) * 128)
            kj = zk_ref[0, pl.ds(r0, bk), sl].astype(BF16)
            vj = jnp.concatenate([zv_ref[0, pl.ds(r0, bk), sl].astype(BF16), ones], axis=1)
            nbias = jnp.where(valid, -ALIBI[g] * distf, NEG)
            for m in range(2):
                a = 2 * g + m
                s = _dot_nt(q_sc[a], kj) + nbias
                m_new = jnp.maximum(m_sc[a], jnp.max(s, axis=-1, keepdims=True))
                p = jnp.exp(s - m_new).astype(BF16)
                acc_sc[a] = acc_sc[a] * jnp.exp(m_sc[a] - m_new) + _dot(p, vj)
                m_sc[a] = m_new
        return carry

    lax.fori_loop(0, _num_kv_blocks(i, bk), body, 0)
    for g in range(H_DIFF):
        o0 = acc_sc[2 * g]
        o1 = acc_sc[2 * g + 1]
        diff = o0[:, :DV_DIFF] / o0[:, DV_DIFF:] - lam * (o1[:, :DV_DIFF] / o1[:, DV_DIFF:])
        mix_ref[0, :, 512 + g * 128:512 + (g + 1) * 128] = (
            _rms(diff, dgn[g:g + 1]) * (1.0 - lam_init)).astype(BF16)

    @pl.when(i == nc - 1)
    def _():
        st_ref[0] = s_ref[...]


def even_prompt_mixer(z3, ret_gn, diff_gn, lam_vecs, lam_init, pad):
    B, Lp, _ = z3.shape
    nc = Lp // CHUNK
    return pl.pallas_call(
        functools.partial(_even_prompt_body, lam_init=lam_init, pad=pad, nc=nc),
        grid=(B, nc),
        in_specs=[pl.BlockSpec((1, CHUNK, 1536), lambda b, i: (b, i, 0)),
                  pl.BlockSpec((1, CHUNK, 512), lambda b, i: (b, i, 3)),
                  pl.BlockSpec((1, Lp, 512), lambda b, i: (b, 0, 4)),
                  pl.BlockSpec((1, Lp, 512), lambda b, i: (b, 0, 5)),
                  pl.BlockSpec((H_RET, DV_RET), lambda b, i: (0, 0)),
                  pl.BlockSpec((H_DIFF, DV_DIFF), lambda b, i: (0, 0)),
                  pl.BlockSpec((4, DH_DIFF), lambda b, i: (0, 0))],
        out_specs=[pl.BlockSpec((1, CHUNK, 1024), lambda b, i: (b, i, 0)),
                   pl.BlockSpec((1, 2, 128, 128), lambda b, i: (b, 0, 0, 0))],
        out_shape=[jax.ShapeDtypeStruct((B, Lp, 1024), BF16),
                   jax.ShapeDtypeStruct((B, 2, 128, 128), F32)],
        scratch_shapes=[pltpu.VMEM((2, 128, 128), F32),
                        pltpu.VMEM((2 * H_DIFF, CHUNK, 128), BF16),
                        pltpu.VMEM((2 * H_DIFF, CHUNK, 1), F32),
                        pltpu.VMEM((2 * H_DIFF, CHUNK, 2 * DV_DIFF), F32)],
        compiler_params=_params(2),
        name="even_prompt_mixer",
    )(z3, z3, z3, z3, ret_gn, diff_gn, lam_vecs)


def _even_sample_body(pt_ref, zr_ref, zq_ref, zk_ref, zv_ref, st_in_ref, rgn_ref, dgn_ref, lam_ref, *rest,
                      P, n_steps, n_pages, lam_init, s_real):
    kp_refs = rest[:P]
    vp_refs = rest[P:2 * P]
    mix_ref, st_out_ref, q_sc, m_sc, l_sc, acc_sc = rest[2 * P:]
    j = pl.program_id(1)
    spad = SROWS - s_real
    row = lax.broadcasted_iota(jnp.int32, (2 * SROWS, 1), 0)
    lane = lax.broadcasted_iota(jnp.int32, (1, LANES), 1)
    rr = row & (SROWS - 1)

    @pl.when(j == 0)
    def _():
        first = lane < DH_DIFF
        for g in range(H_DIFF):
            qg = zq_ref[0, :, g * 128:(g + 1) * 128] * (DH_DIFF ** -0.5)
            q_sc[g] = jnp.concatenate([jnp.where(first, qg, 0.0), jnp.where(first, 0.0, qg)], axis=0).astype(BF16)
        m_sc[...] = jnp.full_like(m_sc, NEG)
        l_sc[...] = jnp.zeros_like(l_sc)
        acc_sc[...] = jnp.zeros_like(acc_sc)

    kcol = lax.broadcasted_iota(jnp.int32, (1, P * PAGE), 1)
    dist = ((n_pages * PAGE - spad) + rr - j * (P * PAGE) - kcol).astype(F32)
    scores = []
    for g in range(H_DIFF):
        qg = q_sc[g]
        s = jnp.concatenate(
            [_dot_nt(qg, kp_refs[r][pl.ds(g, PAGE, stride=H_DIFF), :].astype(BF16)) for r in range(P)], axis=1)
        scores.append(s - ALIBI[g] * dist)
    stats = [_softmax_step(m_sc[g], l_sc[g], scores[g]) for g in range(H_DIFF)]
    for g in range(H_DIFF):
        m_new, l_new, corr, p = stats[g]
        pb = p.astype(BF16)
        pv = None
        for r in range(P):
            t = _dot(pb[:, r * PAGE:(r + 1) * PAGE], vp_refs[r][pl.ds(g, PAGE, stride=H_DIFF), :].astype(BF16))
            pv = t if pv is None else pv + t
        m_sc[g] = m_new
        l_sc[g] = l_new
        acc_sc[g] = acc_sc[g] * corr + pv

    @pl.when(j == n_steps - 1)
    def _():
        lam = _diff_lambda(lam_ref, lam_init)
        dgn = dgn_ref[...]
        valid = jnp.logical_and(lane >= spad, lane <= rr)
        dnew = (rr - lane).astype(F32)
        for g in range(H_DIFF):
            sl = slice(g * 128, (g + 1) * 128)
            kn = _pad_rows(zk_ref[0, :, sl], PAGE).astype(BF16)
            vn = _pad_rows(zv_ref[0, :, sl], PAGE).astype(BF16)
            s = jnp.where(valid, _dot_nt(q_sc[g], kn) - ALIBI[g] * dnew, NEG)
            _, l_new, corr, p = _softmax_step(m_sc[g], l_sc[g], s)
            o = (acc_sc[g] * corr + _dot(p.astype(BF16), vn)) / l_new
            diff = o[:SROWS] - lam * o[SROWS:]
            mix_ref[0, :, 512 + g * 128:512 + (g + 1) * 128] = (
                _rms(diff, dgn[g:g + 1]) * (1.0 - lam_init)).astype(BF16)
        zr = _pad_rows(zr_ref[0], CHUNK)
        outs, new_pairs = _retention_chunk(zr, [st_in_ref[0, 0], st_in_ref[0, 1]], SROWS - s_real, s_real)
        st_out_ref[0, 0] = new_pairs[0]
        st_out_ref[0, 1] = new_pairs[1]
        ret = _retention_mix([x[:SROWS] for x in outs], zr[:SROWS], rgn_ref[...])
        for h in range(H_RET):
            mix_ref[0, :, h * 128:(h + 1) * 128] = ret[h].astype(BF16)


def even_sample_mixer(z3, cache_k, cache_v, e, page_table, st_in, ret_gn, diff_gn, lam_vecs, lam_init, s_real):
    DB = z3.shape[0]
    n_pages = page_table.shape[1]
    P = next(c for c in (16, 8, 4, 2, 1) if n_pages % c == 0)
    n_steps = n_pages // P
    page_specs = [pl.BlockSpec((None, None, PAGE * H_DIFF, 128),
                               (lambda b, j, pt, r=r: (e, pt[b * n_pages + j * P + r], 0, 0)))
                  for r in range(P)]
    grid_spec = pltpu.PrefetchScalarGridSpec(
        num_scalar_prefetch=1,
        grid=(DB, n_steps),
        in_specs=[pl.BlockSpec((1, SROWS, 1536), lambda b, j, pt: (b, 0, 0)),
                  pl.BlockSpec((1, SROWS, 512), lambda b, j, pt: (b, 0, 3)),
                  pl.BlockSpec((1, SROWS, 512), lambda b, j, pt: (b, 0, 4)),
                  pl.BlockSpec((1, SROWS, 512), lambda b, j, pt: (b, 0, 5)),
                  pl.BlockSpec((1, 2, 128, 128), lambda b, j, pt: (b, 0, 0, 0)),
                  pl.BlockSpec((H_RET, DV_RET), lambda b, j, pt: (0, 0)),
                  pl.BlockSpec((H_DIFF, DV_DIFF), lambda b, j, pt: (0, 0)),
                  pl.BlockSpec((4, DH_DIFF), lambda b, j, pt: (0, 0))] + page_specs + page_specs,
        out_specs=[pl.BlockSpec((1, SROWS, 1024), lambda b, j, pt: (b, 0, 0)),
                   pl.BlockSpec((1, 2, 128, 128), lambda b, j, pt: (b, 0, 0, 0))],
        scratch_shapes=[pltpu.VMEM((H_DIFF, 2 * SROWS, 128), BF16),
                        pltpu.VMEM((H_DIFF, 2 * SROWS, 1), F32),
                        pltpu.VMEM((H_DIFF, 2 * SROWS, 1), F32),
                        pltpu.VMEM((H_DIFF, 2 * SROWS, DV_DIFF), F32)],
    )
    return pl.pallas_call(
        functools.partial(_even_sample_body, P=P, n_steps=n_steps, n_pages=n_pages, lam_init=lam_init,
                          s_real=s_real),
        grid_spec=grid_spec,
        out_shape=[jax.ShapeDtypeStruct((DB, SROWS, 1024), BF16),
                   jax.ShapeDtypeStruct((DB, 2, 128, 128), F32)],
        compiler_params=_params(2),
        name="even_sample_mixer",
    )(page_table.reshape(-1), z3, z3, z3, z3, st_in, ret_gn, diff_gn, lam_vecs,
      *([cache_k] * P), *([cache_v] * P))


def _mla_prep_prompt_body(zc_ref, zg_ref, gcq_ref, gckv_ref, wqa_ref, wqb_ref, wuk_ref, e_ref, wuv_ref,
                          cq_t_ref, sq_t_ref, ck_t_ref, sk_t_ref,
                          ckv_ref, kpe_ref, q_ref, k_ref, v_ref):
    zc = zc_ref[...]
    cq = _rms(zc[:, :D_CQ], gcq_ref[...]).astype(BF16)
    ckv = _rms(zc[:, D_CQ:], gckv_ref[...])
    ckv_ref[...] = ckv
    ckvb = ckv.astype(BF16)
    ct = jnp.concatenate([cq_t_ref[...]] * H_MLA, axis=1)
    st = jnp.concatenate([sq_t_ref[...]] * H_MLA, axis=1)
    q = (_dot(cq, wqa_ref[...]) * ct + _dot(cq, wqb_ref[...]) * st) * MLA_SCALE
    q_ref[...] = q.astype(BF16)
    zg = zg_ref[...]
    kpe = zg[:, G_KPA:G_KPA + D_ROPE] * ck_t_ref[...] + zg[:, G_KPB:G_KPB + D_ROPE] * sk_t_ref[...]
    kpe_ref[...] = kpe
    k_ref[...] = (_dot(ckvb, wuk_ref[...]) + _dot(kpe.astype(BF16), e_ref[...])).astype(BF16)
    v_ref[...] = _dot(ckvb, wuv_ref[...]).astype(BF16)


def mla_prep_prompt(z, w, tabs, tm, tiles_per_seq):
    M = z.shape[0]
    full = lambda a: pl.BlockSpec(a.shape, lambda i: (0,) * a.ndim)
    tab = lambda a: pl.BlockSpec((tm, a.shape[1]), lambda i: (i % tiles_per_seq, 0))
    weights = [w["g_cq"], w["g_ckv"], w["wqa"], w["wqb"], w["wuk_pad"], w["e_place"], w["wuv"]]
    tables = [tabs["cq"], tabs["sq"], tabs["ck"], tabs["sk"]]
    return pl.pallas_call(
        _mla_prep_prompt_body,
        grid=(M // tm,),
        in_specs=[pl.BlockSpec((tm, 512), lambda i: (i, 0)),
                  pl.BlockSpec((tm, LANES), lambda i: (i, O_GATE // LANES))]
                 + [full(a) for a in weights] + [tab(a) for a in tables],
        out_specs=[pl.BlockSpec((tm, D_CKV), lambda i: (i, 0)),
                   pl.BlockSpec((tm, D_ROPE), lambda i: (i, 0)),
                   pl.BlockSpec((tm, H_MLA * 128), lambda i: (i, 0)),
                   pl.BlockSpec((tm, H_MLA * 128), lambda i: (i, 0)),
                   pl.BlockSpec((tm, H_MLA * DV_MLA), lambda i: (i, 0))],
        out_shape=[jax.ShapeDtypeStruct((M, D_CKV), F32),
                   jax.ShapeDtypeStruct((M, D_ROPE), F32),
                   jax.ShapeDtypeStruct((M, H_MLA * 128), BF16),
                   jax.ShapeDtypeStruct((M, H_MLA * 128), BF16),
                   jax.ShapeDtypeStruct((M, H_MLA * DV_MLA), BF16)],
        compiler_params=_params(1),
        name="mla_prep_prompt",
    )(z, z, *weights, *tables)


def _mla_prep_sample_body(zc_ref, zg_ref, gcq_ref, gckv_ref, wqn_ref, wpa_ref, wpb_ref, wukt_ref,
                          cp_t_ref, sp_t_ref, ck_t_ref, sk_t_ref,
                          ckv_ref, kpe_ref, qlat_ref, qpe_ref):
    zc = zc_ref[...]
    cq = _rms(zc[:, :D_CQ], gcq_ref[...]).astype(BF16)
    ckv_ref[...] = _rms(zc[:, D_CQ:], gckv_ref[...])
    zg = zg_ref[...]
    kpe_ref[...] = zg[:, G_KPA:G_KPA + D_ROPE] * ck_t_ref[...] + zg[:, G_KPB:G_KPB + D_ROPE] * sk_t_ref[...]
    qn = _dot(cq, wqn_ref[...]) * MLA_SCALE
    col = lax.broadcasted_iota(jnp.int32, (1, H_MLA * D_NOPE), 1)
    for h in range(H_MLA):
        qh = jnp.where(jnp.right_shift(col, 6) == h, qn, 0.0).astype(BF16)
        qlat_ref[:, h * D_CKV:(h + 1) * D_CKV] = _dot(qh, wukt_ref[...]).astype(BF16)
    qpe = (_dot(cq, wpa_ref[...]) * cp_t_ref[...] + _dot(cq, wpb_ref[...]) * sp_t_ref[...]) * MLA_SCALE
    qpe_ref[...] = qpe.astype(BF16)


def mla_prep_sample(z, w, tabs):
    M = z.shape[0]
    full = lambda a: pl.BlockSpec(a.shape, lambda i: (0,) * a.ndim)
    weights = [w["g_cq"], w["g_ckv"], w["wq_nope"], w["wpe_a"], w["wpe_b"], w["wuk_t"]]
    tables = [tabs["cp"], tabs["sp"], tabs["ck"], tabs["sk"]]
    return pl.pallas_call(
        _mla_prep_sample_body,
        grid=(1,),
        in_specs=[pl.BlockSpec((M, 512), lambda i: (0, 0)),
                  pl.BlockSpec((M, LANES), lambda i: (0, O_GATE // LANES))]
                 + [full(a) for a in weights] + [full(a) for a in tables],
        out_specs=[pl.BlockSpec((M, D_CKV), lambda i: (0, 0)),
                   pl.BlockSpec((M, D_ROPE), lambda i: (0, 0)),
                   pl.BlockSpec((M, H_MLA * D_CKV), lambda i: (0, 0)),
                   pl.BlockSpec((M, H_MLA * D_ROPE), lambda i: (0, 0))],
        out_shape=[jax.ShapeDtypeStruct((M, D_CKV), F32),
                   jax.ShapeDtypeStruct((M, D_ROPE), F32),
                   jax.ShapeDtypeStruct((M, H_MLA * D_CKV), BF16),
                   jax.ShapeDtypeStruct((M, H_MLA * D_ROPE), BF16)],
        compiler_params=_params(1),
        name="mla_prep_sample",
    )(z, z, *weights, *tables)


def _store_mlstm_state(c_ref, n_ref, m_ref, cexts, ms):
    lane = lax.broadcasted_iota(jnp.int32, (1, LANES), 1)
    mrow = jnp.zeros((1, LANES), F32)
    for h in range(H_MLSTM):
        c_ref[0, h] = cexts[h][:, :DV_MLSTM]
        n_ref[0, h:h + 1, :] = cexts[h][:, DV_MLSTM:].T[0:1, :]
        mrow = jnp.where(lane == h, ms[h], mrow)
    m_ref[0] = mrow


def _odd_prompt_body(q_ref, k_ref, v_ref, mq_ref, mk_ref, mv_ref, mo_ref, gt_ref, bvec_ref, gn_ref,
                     mix_ref, c_ref, n_ref, m_ref, cext_ref, ms_ref, m_sc, acc_sc, *, pad, nc):
    i = pl.program_id(1)

    @pl.when(i == 0)
    def _():
        cext_ref[...] = jnp.zeros_like(cext_ref)
        ms_ref[...] = jnp.zeros_like(ms_ref)

    lane = lax.broadcasted_iota(jnp.int32, (1, LANES), 1)
    lo = lane < DV_MLA
    m_sc[...] = jnp.full_like(m_sc, NEG)
    acc_sc[...] = jnp.zeros_like(acc_sc)
    bk = k_ref.shape[1] if k_ref.shape[1] < KV_BLOCK else KV_BLOCK
    ones = jnp.ones((bk, LANES), BF16)

    def body(jb, carry):
        r0, valid, _ = _causal_block(jb, i, k_ref.shape[1], bk, pad)
        nmask = jnp.where(valid, 0.0, NEG)
        for p in range(H_MLA // 2):
            vp = v_ref[0, pl.ds(r0, bk), p * 128:(p + 1) * 128]
            for s in range(2):
                a = 2 * p + s
                sl = slice(a * 128, (a + 1) * 128)
                sc = _dot_nt(q_ref[0, :, sl], k_ref[0, pl.ds(r0, bk), sl]) + nmask
                m_new = jnp.maximum(m_sc[a], jnp.max(sc, axis=-1, keepdims=True))
                pr = jnp.exp(sc - m_new).astype(BF16)
                vh = jnp.where(lo, vp, jnp.zeros_like(vp)) if s == 0 else jnp.where(lo, jnp.zeros_like(vp), vp)
                acc_sc[a] = acc_sc[a] * jnp.exp(m_sc[a] - m_new) + _dot(pr, jnp.concatenate([vh, ones], axis=1))
                m_sc[a] = m_new
        return carry

    lax.fori_loop(0, _num_kv_blocks(i, bk), body, 0)
    for p in range(H_MLA // 2):
        oa = acc_sc[2 * p]
        ob = acc_sc[2 * p + 1]
        mix_ref[0, :, p * 128:(p + 1) * 128] = (
            oa[:, :LANES] / oa[:, LANES:] + ob[:, :LANES] / ob[:, LANES:]).astype(BF16)

    row = lax.broadcasted_iota(jnp.int32, (CHUNK, 1), 0)
    valid = row + i * CHUNK >= pad
    cexts = [cext_ref[h] for h in range(H_MLSTM)]
    ms = [ms_ref[h:h + 1, 0:1] for h in range(H_MLSTM)]
    outs, ncs, nms = _mlstm_chunk(mq_ref[0], mk_ref[0], mv_ref[0], mo_ref[0], gt_ref[0], bvec_ref[...], valid,
                                  cexts, ms, gn_ref[...])
    for h in range(H_MLSTM):
        mix_ref[0, :, 512 + h * 128:512 + (h + 1) * 128] = outs[h].astype(BF16)
        cext_ref[h] = ncs[h]
        ms_ref[h:h + 1, :] = jnp.broadcast_to(nms[h], (1, LANES))

    @pl.when(i == nc - 1)
    def _():
        _store_mlstm_state(c_ref, n_ref, m_ref, ncs, nms)


def odd_prompt_mixer(q3, k3, v3, z3, bvec, gn, pad):
    B, Lp, _ = z3.shape
    nc = Lp // CHUNK
    zspec = lambda blk: pl.BlockSpec((1, CHUNK, 512), lambda b, i: (b, i, blk))
    return pl.pallas_call(
        functools.partial(_odd_prompt_body, pad=pad, nc=nc),
        grid=(B, nc),
        in_specs=[pl.BlockSpec((1, CHUNK, H_MLA * 128), lambda b, i: (b, i, 0)),
                  pl.BlockSpec((1, Lp, H_MLA * 128), lambda b, i: (b, 0, 0)),
                  pl.BlockSpec((1, Lp, H_MLA * DV_MLA), lambda b, i: (b, 0, 0)),
                  zspec(O_MQ // 512), zspec(O_MK // 512), zspec(O_MV // 512), zspec(O_MO // 512),
                  pl.BlockSpec((1, CHUNK, LANES), lambda b, i: (b, i, O_GATE // LANES)),
                  pl.BlockSpec((1, LANES), lambda b, i: (0, 0)),
                  pl.BlockSpec((H_MLSTM, DV_MLSTM), lambda b, i: (0, 0))],
        out_specs=[pl.BlockSpec((1, CHUNK, 1024), lambda b, i: (b, i, 0)),
                   pl.BlockSpec((1, H_MLSTM, 128, 128), lambda b, i: (b, 0, 0, 0)),
                   pl.BlockSpec((1, H_MLSTM, 128), lambda b, i: (b, 0, 0)),
                   pl.BlockSpec((1, 1, LANES), lambda b, i: (b, 0, 0))],
        out_shape=[jax.ShapeDtypeStruct((B, Lp, 1024), BF16),
                   jax.ShapeDtypeStruct((B, H_MLSTM, 128, 128), F32),
                   jax.ShapeDtypeStruct((B, H_MLSTM, 128), F32),
                   jax.ShapeDtypeStruct((B, 1, LANES), F32)],
        scratch_shapes=[pltpu.VMEM((H_MLSTM, 128, 256), F32),
                        pltpu.VMEM((8, LANES), F32),
                        pltpu.VMEM((H_MLA, CHUNK, 1), F32),
                        pltpu.VMEM((H_MLA, CHUNK, 2 * LANES), F32)],
        compiler_params=_params(2),
        name="odd_prompt_mixer",
    )(q3, k3, v3, z3, z3, z3, z3, z3, bvec, gn)


def _odd_sample_body(pt_ref, qlat_ref, qpe_ref, ckvn_ref, kpen_ref, mq_ref, mk_ref, mv_ref, mo_ref, gt_ref,
                     cin_ref, nin_ref, min_ref, bvec_ref, gn_ref, wuv_ref, *rest,
                     P, n_steps, s_real):
    cp_refs = rest[:P]
    kp_refs = rest[P:2 * P]
    mix_ref, c_ref, n_ref, m_ref, m_sc, l_sc, acc_sc = rest[2 * P:]
    j = pl.program_id(1)
    nrow = H_MLA * SROWS
    row = lax.broadcasted_iota(jnp.int32, (nrow, 1), 0)
    lane = lax.broadcasted_iota(jnp.int32, (1, LANES), 1)
    rr = row & (SROWS - 1)

    @pl.when(j == 0)
    def _():
        m_sc[...] = jnp.full_like(m_sc, NEG)
        l_sc[...] = jnp.zeros_like(l_sc)
        acc_sc[...] = jnp.zeros_like(acc_sc)

    def update(s, pages):
        m_new, l_new, corr, p = _softmax_step(m_sc[...], l_sc[...], s)
        pb = p.astype(BF16)
        pv = None
        for r, cb in enumerate(pages):
            t = _dot(pb[:, r * PAGE:(r + 1) * PAGE], cb)
            pv = t if pv is None else pv + t
        m_sc[...] = m_new
        l_sc[...] = l_new
        acc_sc[...] = acc_sc[...] * corr + pv

    qlat = qlat_ref[0]
    qpe = qpe_ref[0]
    pages = [cp_refs[r][...].astype(BF16) for r in range(P)]
    update(jnp.concatenate([_dot_nt(qlat, pages[r]) + _dot(qpe, kp_refs[r][...].astype(BF16))
                            for r in range(P)], axis=1), pages)

    @pl.when(j == n_steps - 1)
    def _():
        cn = _pad_rows(ckvn_ref[0], PAGE).astype(BF16)
        kn = _pad_rows(kpen_ref[0], PAGE).astype(BF16)
        s = _dot_nt(qlat, cn) + _dot_nt(qpe, kn)
        update(jnp.where(jnp.logical_and(lane >= SROWS - s_real, lane <= rr), s, NEG), [cn])
        o_lat = (acc_sc[...] / l_sc[...]).astype(BF16)
        r_all = _dot(o_lat, wuv_ref[...])
        col = lax.broadcasted_iota(jnp.int32, (1, H_MLA * DV_MLA), 1)
        mla = jnp.zeros((SROWS, H_MLA * DV_MLA), F32)
        for h in range(H_MLA):
            mla = jnp.where(jnp.right_shift(col, 6) == h, r_all[h * SROWS:(h + 1) * SROWS], mla)
        mix_ref[0, :, 0:512] = mla.astype(BF16)

        rowc = lax.broadcasted_iota(jnp.int32, (CHUNK, 1), 0)
        valid = jnp.logical_and(rowc >= SROWS - s_real, rowc < SROWS)
        cexts, ms = [], []
        for h in range(H_MLSTM):
            ncol = jnp.broadcast_to(nin_ref[0, h:h + 1, :], (LANES, DK_MLSTM)).T
            cexts.append(jnp.concatenate([cin_ref[0, h], ncol], axis=1))
            ms.append(min_ref[0, 0:1, h:h + 1])
        outs, ncs, nms = _mlstm_chunk(_pad_rows(mq_ref[0], CHUNK), _pad_rows(mk_ref[0], CHUNK),
                                      _pad_rows(mv_ref[0], CHUNK), _pad_rows(mo_ref[0], CHUNK),
                                      _pad_rows(gt_ref[0], CHUNK), bvec_ref[...], valid, cexts, ms, gn_ref[...])
        for h in range(H_MLSTM):
            mix_ref[0, :, 512 + h * 128:512 + (h + 1) * 128] = outs[h][:SROWS].astype(BF16)
        _store_mlstm_state(c_ref, n_ref, m_ref, ncs, nms)


def odd_sample_mixer(qlat3, qpe3, ckvn3, kpen3, z3, cache_ckv, cache_kpe, od, page_table, c_in, n_in, m_in,
                     bvec, gn, wuv, s_real):
    DB = z3.shape[0]
    n_pages = page_table.shape[1]
    P = next(c for c in (16, 8, 4, 2, 1) if n_pages % c == 0)
    n_steps = n_pages // P
    nrow = H_MLA * SROWS
    pidx = lambda b, j, pt, r: (od, pt[b * n_pages + j * P + r], 0, 0)
    ckv_specs = [pl.BlockSpec((None, None, PAGE, D_CKV), functools.partial(pidx, r=r)) for r in range(P)]
    kpe_specs = [pl.BlockSpec((None, None, D_ROPE, PAGE), functools.partial(pidx, r=r)) for r in range(P)]
    zspec = lambda blk: pl.BlockSpec((1, SROWS, 512), lambda b, j, pt: (b, 0, blk))
    grid_spec = pltpu.PrefetchScalarGridSpec(
        num_scalar_prefetch=1,
        grid=(DB, n_steps),
        in_specs=[pl.BlockSpec((1, nrow, D_CKV), lambda b, j, pt: (b, 0, 0)),
                  pl.BlockSpec((1, nrow, D_ROPE), lambda b, j, pt: (b, 0, 0)),
                  pl.BlockSpec((1, SROWS, D_CKV), lambda b, j, pt: (b, 0, 0)),
                  pl.BlockSpec((1, SROWS, D_ROPE), lambda b, j, pt: (b, 0, 0)),
                  zspec(O_MQ // 512), zspec(O_MK // 512), zspec(O_MV // 512), zspec(O_MO // 512),
                  pl.BlockSpec((1, SROWS, LANES), lambda b, j, pt: (b, 0, O_GATE // LANES)),
                  pl.BlockSpec((1, H_MLSTM, 128, 128), lambda b, j, pt: (b, 0, 0, 0)),
                  pl.BlockSpec((1, H_MLSTM, 128), lambda b, j, pt: (b, 0, 0)),
                  pl.BlockSpec((1, 1, LANES), lambda b, j, pt: (b, 0, 0)),
                  pl.BlockSpec((1, LANES), lambda b, j, pt: (0, 0)),
                  pl.BlockSpec((H_MLSTM, DV_MLSTM), lambda b, j, pt: (0, 0)),
                  pl.BlockSpec((D_CKV, H_MLA * DV_MLA), lambda b, j, pt: (0, 0))] + ckv_specs + kpe_specs,
        out_specs=[pl.BlockSpec((1, SROWS, 1024), lambda b, j, pt: (b, 0, 0)),
                   pl.BlockSpec((1, H_MLSTM, 128, 128), lambda b, j, pt: (b, 0, 0, 0)),
                   pl.BlockSpec((1, H_MLSTM, 128), lambda b, j, pt: (b, 0, 0)),
                   pl.BlockSpec((1, 1, LANES), lambda b, j, pt: (b, 0, 0))],
        scratch_shapes=[pltpu.VMEM((nrow, 1), F32),
                        pltpu.VMEM((nrow, 1), F32),
                        pltpu.VMEM((nrow, D_CKV), F32)],
    )
    return pl.pallas_call(
        functools.partial(_odd_sample_body, P=P, n_steps=n_steps, s_real=s_real),
        grid_spec=grid_spec,
        out_shape=[jax.ShapeDtypeStruct((DB, SROWS, 1024), BF16),
                   jax.ShapeDtypeStruct((DB, H_MLSTM, 128, 128), F32),
                   jax.ShapeDtypeStruct((DB, H_MLSTM, 128), F32),
                   jax.ShapeDtypeStruct((DB, 1, LANES), F32)],
        compiler_params=_params(2),
        name="odd_sample_mixer",
    )(page_table.reshape(-1), qlat3, qpe3, ckvn3, kpen3, z3, z3, z3, z3, z3, c_in, n_in, m_in, bvec, gn, wuv,
      *([cache_ckv] * P), *([cache_kpe] * P))


def _odd_w_in(w):
    D = w.shape[0]
    kp1, kp2 = w[:, 512:528], w[:, 528:544]
    gate = jnp.concatenate([kp1, kp2, kp2, kp1, w[:, 2592:2596], w[:, 2596:2600],
                            jnp.zeros((D, LANES - 72), w.dtype)], axis=1)
    return jnp.concatenate([w[:, 0:512], w[:, 544:2592], gate], axis=1).astype(BF16)


def _mla_weights(g_cq, g_ckv, w_uq, w_uk, w_uv):
    C = w_uq.shape[0]
    nope, pe1, pe2 = w_uq[:, :, :D_NOPE], w_uq[:, :, D_NOPE:D_NOPE + 16], w_uq[:, :, D_NOPE + 16:]
    z64 = jnp.zeros((C, H_MLA, D_NOPE), w_uq.dtype)
    z32 = jnp.zeros((C, H_MLA, 32), w_uq.dtype)
    wqa = jnp.concatenate([nope, pe1, pe2, z32], axis=2).reshape(C, H_MLA * 128)
    wqb = jnp.concatenate([z64, pe2, pe1, z32], axis=2).reshape(C, H_MLA * 128)
    wuk_pad = jnp.concatenate([w_uk, jnp.zeros((C, H_MLA, 64), w_uk.dtype)], axis=2).reshape(C, H_MLA * 128)
    e1 = jnp.concatenate([jnp.zeros((D_ROPE, D_NOPE), F32), jnp.eye(D_ROPE, dtype=F32),
                          jnp.zeros((D_ROPE, 32), F32)], axis=1)
    return {
        "g_cq": g_cq.reshape(1, -1), "g_ckv": g_ckv.reshape(1, -1),
        "wqa": wqa.astype(BF16), "wqb": wqb.astype(BF16), "wuk_pad": wuk_pad.astype(BF16),
        "e_place": jnp.tile(e1, (1, H_MLA)).astype(BF16),
        "wuv": w_uv.reshape(C, H_MLA * DV_MLA).astype(BF16),
        "wq_nope": nope.reshape(C, H_MLA * D_NOPE).astype(BF16),
        "wpe_a": jnp.concatenate([pe1, pe2], axis=2).reshape(C, H_MLA * D_ROPE).astype(BF16),
        "wpe_b": jnp.concatenate([pe2, pe1], axis=2).reshape(C, H_MLA * D_ROPE).astype(BF16),
        "wuk_t": w_uk.transpose(1, 2, 0).reshape(H_MLA * D_NOPE, C).astype(BF16),
    }


def _rope_tables(pos):
    half = D_ROPE // 2
    freq = ROPE_THETA ** (-jnp.arange(half, dtype=F32) / half)
    ang = pos.astype(F32)[:, None] * freq[None, :]
    cos, sin = jnp.cos(ang), jnp.sin(ang)
    n = pos.shape[0]
    ck = jnp.concatenate([cos, cos], axis=1)
    sk = jnp.concatenate([-sin, sin], axis=1)
    cq = jnp.concatenate([jnp.ones((n, D_NOPE), F32), ck, jnp.zeros((n, 32), F32)], axis=1)
    sq = jnp.concatenate([jnp.zeros((n, D_NOPE), F32), sk, jnp.zeros((n, 32), F32)], axis=1)
    return {"ck": ck, "sk": sk, "cq": cq, "sq": sq,
            "cp": jnp.tile(ck, (1, H_MLA)), "sp": jnp.tile(sk, (1, H_MLA))}


def _lambda_init(layer):
    return 0.8 - 0.6 * math.exp(-0.3 * layer)


def kernel(x_prompt, x_sample, cache_diff_k, cache_diff_v, cache_mla_ckv, cache_mla_kpe, page_table, state_ret, state_mlstm_c, state_mlstm_n, state_mlstm_m, state_ffn_conv, meta_tokens, norm_mix, norm_ffn, norm_final, w_in_even, w_out_even, ret_gn, diff_gn, lam_q1, lam_k1, lam_q2, lam_k2, w_in_odd, w_out_odd, mla_g_cq, mla_g_ckv, mla_w_uq, mla_w_uk, mla_w_uv, mlstm_b_i, mlstm_b_f, mlstm_gn, ffn_w_gate, ffn_w_up, ffn_conv_w, ffn_conv_b, ffn_w_down):
    B, SEQ, D = x_prompt.shape
    DB, S, _ = x_sample.shape
    depth = norm_mix.shape[0]
    F = ffn_w_gate.shape[2]
    n_pages = page_table.shape[1]
    assert S <= SROWS
    L = SEQ + N_META
    pad = (-L) % CHUNK
    Lp = L + pad
    spad = SROWS - S
    MP, MS = B * Lp, DB * SROWS
    tmp = Lp // 2 if (Lp // 2) % 16 == 0 else Lp
    tps = Lp // tmp
    rc = Lp // 4 if Lp % 64 == 0 and Lp >= 1024 else Lp
    tf = 256

    xp = jnp.concatenate([jnp.zeros((B, pad, D), F32),
                          jnp.broadcast_to(meta_tokens[None], (B, N_META, D)), x_prompt], axis=1).reshape(MP, D)
    xs = jnp.concatenate([jnp.zeros((DB, spad, D), F32), x_sample], axis=1).reshape(MS, D)

    ck_pages = cache_diff_k.reshape(cache_diff_k.shape[0], cache_diff_k.shape[1], PAGE * H_DIFF, 2 * DH_DIFF)
    cv_pages = cache_diff_v.reshape(cache_diff_v.shape[0], cache_diff_v.shape[1], PAGE * H_DIFF, DV_DIFF)
    kpe_pages = jnp.swapaxes(cache_mla_kpe, 2, 3)

    pos_p = jnp.maximum(jnp.arange(Lp) - pad, 0)
    pos_s = jnp.tile(jnp.maximum(n_pages * PAGE + jnp.arange(SROWS) - spad, 0), DB)
    tabs_p = _rope_tables(pos_p)
    tabs_s = _rope_tables(pos_s)

    dk_p, dv_p, dk_s, dv_s, ret_p, ret_s = [], [], [], [], [], []
    ckv_p, kpe_p, ckv_s, kpe_s = [], [], [], []
    mc_p, mn_p, mm_p, mc_s, mn_s, mm_s = [], [], [], [], [], []
    cb_p, cb_s = [], []
    for layer in range(depth):
        if layer % 2 == 0:
            e = layer // 2
            li = _lambda_init(layer)
            w_in = w_in_even[e].astype(BF16)
            w_out = w_out_even[e].astype(BF16)
            lam_vecs = jnp.stack([lam_q1[e], lam_k1[e], lam_q2[e], lam_k2[e]])
            zp = norm_matmul(xp, norm_mix[layer], w_in, tmp, 512).reshape(B, Lp, -1)
            zs = norm_matmul(xs, norm_mix[layer], w_in, MS, 512).reshape(DB, SROWS, -1)
            mix_p, st_p = even_prompt_mixer(zp, ret_gn[e], diff_gn[e], lam_vecs, li, pad)
            mix_s, st_s = even_sample_mixer(zs, ck_pages, cv_pages, e, page_table,
                                            state_ret[e].reshape(DB, 2, 128, 128),
                                            ret_gn[e], diff_gn[e], lam_vecs, li, S)
            dk_p.append(zp[:, pad:, 2048:2560].reshape(B, L, H_DIFF, 2 * DH_DIFF))
            dv_p.append(zp[:, pad:, 2560:3072].reshape(B, L, H_DIFF, DV_DIFF))
            dk_s.append(zs[:, spad:, 2048:2560].reshape(DB, S, H_DIFF, 2 * DH_DIFF))
            dv_s.append(zs[:, spad:, 2560:3072].reshape(DB, S, H_DIFF, DV_DIFF))
            ret_p.append(st_p.reshape(B, H_RET, DK_RET, DV_RET))
            ret_s.append(st_s.reshape(DB, H_RET, DK_RET, DV_RET))
        else:
            od = layer // 2
            w_in = _odd_w_in(w_in_odd[od])
            w_out = w_out_odd[od].astype(BF16)
            mw = _mla_weights(mla_g_cq[od], mla_g_ckv[od], mla_w_uq[od], mla_w_uk[od], mla_w_uv[od])
            bvec = jnp.concatenate([jnp.zeros((G_MI,), F32), mlstm_b_i[od], mlstm_b_f[od],
                                    jnp.zeros((LANES - G_MF - H_MLSTM,), F32)]).reshape(1, LANES)
            zp2 = norm_matmul(xp, norm_mix[layer], w_in, tmp, 384)
            zs2 = norm_matmul(xs, norm_mix[layer], w_in, MS, 384)
            ckv_n, kpe_r, q_b, k_b, v_b = mla_prep_prompt(zp2, mw, tabs_p, tmp, tps)
            mix_p, c_p, n_p, m_p = odd_prompt_mixer(
                q_b.reshape(B, Lp, -1), k_b.reshape(B, Lp, -1), v_b.reshape(B, Lp, -1),
                zp2.reshape(B, Lp, -1), bvec, mlstm_gn[od], pad)
            ckv_ns, kpe_rs, qlat, qpe = mla_prep_sample(zs2, mw, tabs_s)
            qlat3 = qlat.reshape(DB, SROWS, H_MLA, D_CKV).transpose(0, 2, 1, 3).reshape(DB, H_MLA * SROWS, D_CKV)
            qpe3 = qpe.reshape(DB, SROWS, H_MLA, D_ROPE).transpose(0, 2, 1, 3).reshape(DB, H_MLA * SROWS, D_ROPE)
            m_in = jnp.pad(state_mlstm_m[od], ((0, 0), (0, LANES - H_MLSTM))).reshape(DB, 1, LANES)
            mix_s, c_s, n_s, m_s = odd_sample_mixer(
                qlat3, qpe3, ckv_ns.reshape(DB, SROWS, -1), kpe_rs.reshape(DB, SROWS, -1),
                zs2.reshape(DB, SROWS, -1), cache_mla_ckv, kpe_pages, od, page_table,
                state_mlstm_c[od], state_mlstm_n[od], m_in, bvec, mlstm_gn[od], mw["wuv"], S)
            ckv_p.append(ckv_n.reshape(B, Lp, -1)[:, pad:])
            kpe_p.append(kpe_r.reshape(B, Lp, -1)[:, pad:])
            ckv_s.append(ckv_ns.reshape(DB, SROWS, -1)[:, spad:])
            kpe_s.append(kpe_rs.reshape(DB, SROWS, -1)[:, spad:])
            mc_p.append(c_p)
            mn_p.append(n_p)
            mm_p.append(m_p[:, 0, :H_MLSTM])
            mc_s.append(c_s)
            mn_s.append(n_s)
            mm_s.append(m_s[:, 0, :H_MLSTM])
        xp = proj_residual(xp, mix_p.reshape(MP, -1), w_out, tmp, Lp, pad)
        xs = proj_residual(xs, mix_s.reshape(MS, -1), w_out, MS, SROWS, spad)
        wg, wu, wd = ffn_w_gate[layer].astype(BF16), ffn_w_up[layer].astype(BF16), ffn_w_down[layer].astype(BF16)
        gstate = jnp.pad(state_ffn_conv[layer], ((0, 0), (SROWS - S - (CONV_W - 1), S), (0, 0))).reshape(MS, F)
        gstate = jnp.pad(gstate, ((HALO, 0), (0, 0)))
        xp, conv_p = conv_ffn(xp, norm_ffn[layer], wg, wu, ffn_conv_w[layer], ffn_conv_b[layer], wd, Lp, rc, tf)
        xs, conv_s = conv_ffn(xs, norm_ffn[layer], wg, wu, ffn_conv_w[layer], ffn_conv_b[layer], wd, MS, MS, tf,
                              gstate=gstate)
        cb_p.append(conv_p)
        cb_s.append(conv_s.reshape(DB, SROWS, F)[:, SROWS - (CONV_W - 1):])

    off = pad + N_META
    xp3 = xp.reshape(B, Lp, D)
    if off % CHUNK == 0 and SEQ % CHUNK == 0:
        y_prompt = final_norm(xp3, norm_final, off, SEQ, CHUNK)
    else:
        y_prompt = final_norm(xp3, norm_final, 0, Lp, CHUNK)[:, off:]
    y_sample = final_norm(xs.reshape(DB, SROWS, D), norm_final, 0, SROWS, SROWS)[:, spad:]

    st = jnp.stack
    return (y_prompt, y_sample,
            st(dk_p), st(dv_p), st(dk_s), st(dv_s),
            st(ckv_p), st(kpe_p), st(ckv_s), st(kpe_s),
            st(ret_p), st(ret_s),
            st(mc_p), st(mn_p), st(mm_p), st(mc_s), st(mn_s), st(mm_s),
            st(cb_p), st(cb_s))
```

```python
import functools
import math

import jax
import jax.numpy as jnp
import numpy as np
from jax import lax
from jax.experimental import pallas as pl
from jax.experimental.pallas import tpu as pltpu

F32 = jnp.float32
BF16 = jnp.bfloat16

N_META = 16
CHUNK = 128
NORM_EPS = 1e-6
NEG = -1e30
H_RET, DK_RET, DV_RET = 4, 64, 128
H_DIFF, DH_DIFF, DV_DIFF = 4, 64, 128
H_MLA, D_NOPE, D_ROPE, DV_MLA = 8, 64, 32, 64
D_CQ, D_CKV = 256, 256
ROPE_THETA = 10000.0
H_MLSTM, DK_MLSTM, DV_MLSTM = 4, 128, 128
CONV_W = 3
PAGE = 128
SROWS = 8
LANES = 128
HALO = 16
KV_BLOCK = 512
ONES_ROWS = 16
VMEM_LIMIT = 56 * 1024 * 1024

O_CQ, O_CKV, O_MQ, O_MK, O_MV, O_MO, O_GATE = 0, 256, 512, 1024, 1536, 2048, 2560
O_IN_PAD = 2688
G_KPA, G_KPB, G_MI, G_MF = 0, 32, 64, 68

RET_LOG_GAMMA = [float(np.log(np.float32(1.0) - np.float32(2.0) ** np.float32(-5.0 - h))) for h in range(H_RET)]
ALIBI = [float(2.0 ** (-8.0 * (h + 1) / H_DIFF)) for h in range(H_DIFF)]
MLA_SCALE = float((D_NOPE + D_ROPE) ** -0.5)


def _params(n_axes, vmem=None):
    return pltpu.CompilerParams(dimension_semantics=("arbitrary",) * n_axes,
                                vmem_limit_bytes=vmem or VMEM_LIMIT)


def _dot(a, b):
    return jnp.dot(a, b, preferred_element_type=F32)


def _dot_nt(a, b):
    return lax.dot_general(a, b, (((1,), (1,)), ((), ())), preferred_element_type=F32)


def _dot_tn(a, b):
    return lax.dot_general(a, b, (((0,), (0,)), ((), ())), preferred_element_type=F32)


def _rms(x, g):
    return x * lax.rsqrt(jnp.mean(x * x, axis=-1, keepdims=True) + NORM_EPS) * g


def _layernorm(x, g):
    xc = x - jnp.mean(x, axis=-1, keepdims=True)
    return xc * lax.rsqrt(jnp.mean(xc * xc, axis=-1, keepdims=True) + NORM_EPS) * g


def _sigmoid(x):
    return 1.0 / (1.0 + jnp.exp(-x))


def _log_sigmoid(x):
    return -(jnp.maximum(-x, 0.0) + jnp.log1p(jnp.exp(-jnp.abs(x))))


def _row_valid(tm, seq_len, pad, i):
    r = lax.broadcasted_iota(jnp.int32, (tm, 1), 0)
    if tm % seq_len == 0 and seq_len & (seq_len - 1) == 0:
        pos = r & (seq_len - 1)
    else:
        assert seq_len % tm == 0
        pos = r + lax.rem(i * tm, seq_len)
    return pos >= pad


def _norm_matmul_body(x_ref, g_ref, w_ref, o_ref, h_ref):
    @pl.when(pl.program_id(1) == 0)
    def _():
        h_ref[...] = _rms(x_ref[...], g_ref[...]).astype(BF16)

    o_ref[...] = _dot(h_ref[...], w_ref[...])


def norm_matmul(x, g, w, tm, tn):
    M, K = x.shape
    N = w.shape[1]
    return pl.pallas_call(
        _norm_matmul_body,
        grid=(M // tm, N // tn),
        in_specs=[pl.BlockSpec((tm, K), lambda i, j: (i, 0)),
                  pl.BlockSpec((1, K), lambda i, j: (0, 0)),
                  pl.BlockSpec((K, tn), lambda i, j: (0, j))],
        out_specs=pl.BlockSpec((tm, tn), lambda i, j: (i, j)),
        out_shape=jax.ShapeDtypeStruct((M, N), F32),
        scratch_shapes=[pltpu.VMEM((tm, K), BF16)],
        compiler_params=_params(2),
        name="norm_matmul",
    )(x, g.reshape(1, K), w)


def _proj_res_body(x_ref, a_ref, w_ref, o_ref, *, tm, seq_len, pad):
    y = x_ref[...] + _dot(a_ref[...], w_ref[...])
    o_ref[...] = jnp.where(_row_valid(tm, seq_len, pad, pl.program_id(0)), y, 0.0)


def proj_residual(x, a, w, tm, seq_len, pad):
    M, D = x.shape
    E = a.shape[1]
    return pl.pallas_call(
        functools.partial(_proj_res_body, tm=tm, seq_len=seq_len, pad=pad),
        grid=(M // tm,),
        in_specs=[pl.BlockSpec((tm, D), lambda i: (i, 0)),
                  pl.BlockSpec((tm, E), lambda i: (i, 0)),
                  pl.BlockSpec((E, D), lambda i: (0, 0))],
        out_specs=pl.BlockSpec((tm, D), lambda i: (i, 0)),
        out_shape=jax.ShapeDtypeStruct((M, D), F32),
        compiler_params=_params(1),
        name="proj_residual",
    )(x, a, w)


def _ffn_body(*refs, tm, rc, has_state):
    if has_state:
        x_ref, gn_ref, wg_ref, wu_ref, cw_ref, cb_ref, wd_ref, gs_ref, y_ref, conv_ref, h_ref = refs
    else:
        x_ref, gn_ref, wg_ref, wu_ref, cw_ref, cb_ref, wd_ref, y_ref, conv_ref, h_ref = refs
        gs_ref = None
    j = pl.program_id(1)
    n_chunks = tm // rc
    K = x_ref.shape[-1]

    @pl.when(j == 0)
    def _():
        h_ref[pl.ds(0, HALO), :] = jnp.zeros((HALO, K), BF16)

        def norm_chunk(c, carry):
            r0 = pl.multiple_of(c * rc, HALO)
            x = x_ref[pl.ds(r0, rc), :]
            h_ref[pl.ds(r0 + HALO, rc), :] = _rms(x, gn_ref[...]).astype(BF16)
            y_ref[pl.ds(r0, rc), :] = x
            return carry

        lax.fori_loop(0, n_chunks, norm_chunk, 0)

    cw = cw_ref[...]
    cb = cb_ref[...]

    def chunk(c, carry):
        r0 = pl.multiple_of(c * rc, HALO)
        hx = h_ref[pl.ds(r0, rc + HALO), :]
        g = _dot(hx, wg_ref[...])
        if has_state:
            g = g + gs_ref[...]
        u = _dot(hx[HALO:], wu_ref[...])
        g0 = g[HALO:]
        g1 = pltpu.roll(g, 1, 0)[HALO:]
        g2 = pltpu.roll(g, 2, 0)[HALO:]
        gc = cb + cw[0:1] * g2 + cw[1:2] * g1 + cw[2:3] * g0
        a = (gc * _sigmoid(gc)) * u
        y_ref[pl.ds(r0, rc), :] += _dot(a.astype(BF16), wd_ref[...])
        if has_state:
            conv_ref[...] = g0
        return carry

    lax.fori_loop(0, n_chunks, chunk, 0)
    if not has_state:
        tail = _dot(h_ref[pl.ds(tm, HALO), :], wg_ref[...])
        conv_ref[0] = tail[HALO - (CONV_W - 1):]


def conv_ffn(x, gn, wg, wu, cw, cb, wd, tm, rc, tf, gstate=None):
    M, D = x.shape
    F = wg.shape[1]
    has_state = gstate is not None
    if has_state:
        assert rc == tm == M
    in_specs = [pl.BlockSpec((tm, D), lambda i, j: (i, 0)),
                pl.BlockSpec((1, D), lambda i, j: (0, 0)),
                pl.BlockSpec((D, tf), lambda i, j: (0, j)),
                pl.BlockSpec((D, tf), lambda i, j: (0, j)),
                pl.BlockSpec((CONV_W, tf), lambda i, j: (0, j)),
                pl.BlockSpec((1, tf), lambda i, j: (0, j)),
                pl.BlockSpec((tf, D), lambda i, j: (j, 0))]
    args = [x, gn.reshape(1, D), wg, wu, cw, cb.reshape(1, F), wd]
    if has_state:
        in_specs.append(pl.BlockSpec((tm + HALO, tf), lambda i, j: (0, j)))
        args.append(gstate)
        conv_spec = pl.BlockSpec((tm, tf), lambda i, j: (0, j))
        conv_shape = jax.ShapeDtypeStruct((M, F), F32)
    else:
        conv_spec = pl.BlockSpec((1, CONV_W - 1, tf), lambda i, j: (i, 0, j))
        conv_shape = jax.ShapeDtypeStruct((M // tm, CONV_W - 1, F), F32)
    return pl.pallas_call(
        functools.partial(_ffn_body, tm=tm, rc=rc, has_state=has_state),
        grid=(M // tm, F // tf),
        in_specs=in_specs,
        out_specs=[pl.BlockSpec((tm, D), lambda i, j: (i, 0)), conv_spec],
        out_shape=[jax.ShapeDtypeStruct((M, D), F32), conv_shape],
        scratch_shapes=[pltpu.VMEM((tm + HALO, D), BF16)],
        compiler_params=_params(2),
        name="conv_ffn",
    )(*args)


def _final_norm_body(x_ref, g_ref, o_ref):
    o_ref[0] = _rms(x_ref[0], g_ref[...])


def final_norm(x3, g, row_off, rows_out, tr):
    B, L, D = x3.shape
    assert row_off % tr == 0 and rows_out % tr == 0
    ob = row_off // tr
    return pl.pallas_call(
        _final_norm_body,
        grid=(B, rows_out // tr),
        in_specs=[pl.BlockSpec((1, tr, D), lambda b, i: (b, i + ob, 0)),
                  pl.BlockSpec((1, D), lambda b, i: (0, 0))],
        out_specs=pl.BlockSpec((1, tr, D), lambda b, i: (b, i, 0)),
        out_shape=jax.ShapeDtypeStruct((B, rows_out, D), F32),
        compiler_params=_params(2),
        name="final_norm",
    )(x3, g.reshape(1, D))


def _retention_chunk(zr, s_pairs, t_off, t_eff):
    T = zr.shape[0]
    lane = lax.broadcasted_iota(jnp.int32, (1, LANES), 1)
    first = lane < DK_RET
    ti = lax.broadcasted_iota(jnp.int32, (T, T), 0)
    si = lax.broadcasted_iota(jnp.int32, (T, T), 1)
    dist = (ti - si).astype(F32)
    tau = lax.broadcasted_iota(jnp.int32, (T, 1), 0).astype(F32) - float(t_off)
    rowi = lax.broadcasted_iota(jnp.int32, (LANES, 1), 0)
    outs, new_pairs = [], []
    for p in range(H_RET // 2):
        qp = zr[:, p * 128:(p + 1) * 128]
        kp = zr[:, 256 + p * 128:256 + (p + 1) * 128] * (DK_RET ** -0.5)
        kb = kp.astype(BF16)
        sb = s_pairs[p].astype(BF16)
        upd = None
        for s in range(2):
            h = 2 * p + s
            lg = RET_LOG_GAMMA[h]
            msk = first if s == 0 else jnp.logical_not(first)
            qh = jnp.where(msk, qp, 0.0).astype(BF16)
            decay = jnp.where(dist >= 0, jnp.exp(lg * jnp.maximum(dist, 0.0)), 0.0)
            qk = (_dot_nt(qh, kb) * decay).astype(BF16)
            vh = zr[:, 512 + h * 128:512 + (h + 1) * 128].astype(BF16)
            inner = _dot(qk, vh)
            cross = _dot(qh, sb) * jnp.exp(lg * (tau + 1.0))
            outs.append(inner + cross)
            kw = jnp.where(msk, kp * jnp.exp(lg * (float(t_eff) - 1.0 - tau)), 0.0).astype(BF16)
            u = _dot_tn(kw, vh)
            upd = u if upd is None else upd + u
        g0 = float(np.exp(np.float32(RET_LOG_GAMMA[2 * p]) * np.float32(t_eff)))
        g1 = float(np.exp(np.float32(RET_LOG_GAMMA[2 * p + 1]) * np.float32(t_eff)))
        new_pairs.append(jnp.where(rowi < DK_RET, g0, g1) * s_pairs[p] + upd)
    return outs, new_pairs


def _retention_mix(outs, zr, rgn):
    res = []
    for h in range(H_RET):
        rg = zr[:, 1024 + h * 128:1024 + (h + 1) * 128]
        gate = rg * _sigmoid(rg)
        res.append(_layernorm(outs[h], rgn[h:h + 1]) * gate)
    return res


def _softmax_step(m, l, s):
    m_new = jnp.maximum(m, jnp.max(s, axis=-1, keepdims=True))
    corr = jnp.exp(m - m_new)
    p = jnp.exp(s - m_new)
    return m_new, l * corr + jnp.sum(p, axis=-1, keepdims=True), corr, p


def _diff_lambda(lam_ref, lam_init):
    lv = lam_ref[...]
    return (jnp.exp(jnp.sum(lv[0:1] * lv[1:2], axis=-1, keepdims=True))
            - jnp.exp(jnp.sum(lv[2:3] * lv[3:4], axis=-1, keepdims=True)) + lam_init)


def _mlstm_gates(gt, bvec, valid):
    gb = gt + bvec
    li = jnp.where(valid, gb, NEG)
    lf = jnp.where(valid, _log_sigmoid(gb), 0.0)
    row = lax.broadcasted_iota(jnp.int32, (CHUNK, 1), 0)
    sh = 1
    while sh < CHUNK:
        lf = lf + jnp.where(row >= sh, pltpu.roll(lf, sh, 0), 0.0)
        sh *= 2
    lane = lax.broadcasted_iota(jnp.int32, (1, LANES), 1)
    x = jnp.where(lane >= G_MF, lf, li)
    return x, x.T


def _mlstm_head(h, x, xt, q, k, v, cext, m_prev):
    bc = x[:, G_MF + h:G_MF + h + 1]
    br = xt[G_MF + h:G_MF + h + 1, :]
    lic = x[:, G_MI + h:G_MI + h + 1]
    lir = xt[G_MI + h:G_MI + h + 1, :]
    b_last = x[CHUNK - 1:CHUNK, G_MF + h:G_MF + h + 1]
    ti = lax.broadcasted_iota(jnp.int32, (CHUNK, CHUNK), 0)
    si = lax.broadcasted_iota(jnp.int32, (CHUNK, CHUNK), 1)
    d = jnp.where(si <= ti, bc - br + lir, NEG)
    inter = bc + m_prev
    m_q = jnp.maximum(inter, jnp.max(d, axis=-1, keepdims=True))
    w = jnp.exp(d - m_q)
    gq = jnp.exp(inter - m_q)
    qb = q.astype(BF16)
    kb = k.astype(BF16)
    vb = v.astype(BF16)
    qk = _dot_nt(qb, kb) * w
    r = _dot(qb, cext.astype(BF16))
    num = _dot(qk.astype(BF16), vb) + gq * r[:, :DV_MLSTM]
    den = jnp.sum(qk, axis=-1, keepdims=True) + gq * r[:, DV_MLSTM:DV_MLSTM + 1]
    h_out = num / jnp.maximum(jnp.abs(den), jnp.exp(-m_q))
    a = b_last - bc + lic
    m_new = jnp.maximum(b_last + m_prev, jnp.max(a, axis=0, keepdims=True))
    ws = jnp.exp(a - m_new)
    gs = jnp.exp(b_last + m_prev - m_new)
    kw = (k * ws).astype(BF16)
    vext = jnp.concatenate([vb, jnp.ones((CHUNK, LANES), BF16)], axis=1)
    return h_out, gs * cext + _dot_tn(kw, vext), m_new


def _mlstm_chunk(mq, mk, mv, mo, gt, bvec, valid, cexts, ms, gn):
    x, xt = _mlstm_gates(gt, bvec, valid)
    outs, ncs, nms = [], [], []
    for h in range(H_MLSTM):
        sl = slice(h * 128, (h + 1) * 128)
        ho, nc, nm = _mlstm_head(h, x, xt, mq[:, sl], mk[:, sl] * (DK_MLSTM ** -0.5), mv[:, sl],
                                 cexts[h], ms[h])
        outs.append(_layernorm(_sigmoid(mo[:, sl]) * ho, gn[h:h + 1]))
        ncs.append(nc)
        nms.append(nm)
    return outs, ncs, nms


def _pad_rows(a, rows):
    return jnp.concatenate([a, jnp.zeros((rows - a.shape[0], a.shape[1]), a.dtype)], axis=0)


def _causal_block(jb, i, seq_rows, bk, pad):
    base = jb * bk
    r0 = pl.multiple_of(jnp.minimum(base, seq_rows - bk), CHUNK)
    kidx = r0 + lax.broadcasted_iota(jnp.int32, (1, bk), 1)
    qidx = i * CHUNK + lax.broadcasted_iota(jnp.int32, (CHUNK, 1), 0)
    dist = qidx - kidx
    valid = jnp.logical_and(dist >= 0, kidx >= jnp.maximum(base, pad))
    return r0, valid, dist


def _causal_block_t(jb, i, seq_rows, bk, pad):
    base = jb * bk
    r0 = pl.multiple_of(jnp.minimum(base, seq_rows - bk), CHUNK)
    kidx = r0 + lax.broadcasted_iota(jnp.int32, (bk, 1), 0)
    qidx = i * CHUNK + lax.broadcasted_iota(jnp.int32, (1, CHUNK), 1)
    dist = qidx - kidx
    valid = jnp.logical_and(dist >= 0, kidx >= jnp.maximum(base, pad))
    return r0, valid, dist


def _num_kv_blocks(i, bk):
    per = bk // CHUNK
    return lax.div(i + per, per)


def _even_prompt_body(zr_ref, zq_ref, zk_ref, zv_ref, rgn_ref, dgn_ref, lam_ref, mix_ref, st_ref, s_ref,
                      q_sc, m_sc, acc_sc, vt_sc, *, lam_init, pad, nc):
    i = pl.program_id(1)

    @pl.when(i == 0)
    def _():
        s_ref[...] = jnp.zeros_like(s_ref)

    zr = zr_ref[0]
    outs, new_pairs = _retention_chunk(zr, [s_ref[0], s_ref[1]], 0, CHUNK)
    s_ref[0] = new_pairs[0]
    s_ref[1] = new_pairs[1]
    ret = _retention_mix(outs, zr, rgn_ref[...])
    for h in range(H_RET):
        mix_ref[0, :, h * 128:(h + 1) * 128] = ret[h].astype(BF16)

    lam = _diff_lambda(lam_ref, lam_init)
    lane = lax.broadcasted_iota(jnp.int32, (1, LANES), 1)
    first = lane < DH_DIFF
    dgn = dgn_ref[...]
    for g in range(H_DIFF):
        qg = zq_ref[0, :, g * 128:(g + 1) * 128] * (DH_DIFF ** -0.5)
        q_sc[2 * g] = jnp.where(first, qg, 0.0).astype(BF16)
        q_sc[2 * g + 1] = jnp.where(first, 0.0, qg).astype(BF16)
    m_sc[...] = jnp.full_like(m_sc, NEG)
    acc_sc[...] = jnp.zeros_like(acc_sc)
    bk = zk_ref.shape[1] if zk_ref.shape[1] < KV_BLOCK else KV_BLOCK

    @pl.when(i == 0)
    def _():
        for g in range(H_DIFF):
            def transpose_chunk(c, carry, g=g):
                r = pl.multiple_of(c * CHUNK, CHUNK)
                vt_sc[g, c, 0:DV_DIFF, :] = zv_ref[0, pl.ds(r, CHUNK), g * 128:(g + 1) * 128].T.astype(BF16)
                vt_sc[g, c, DV_DIFF:, :] = jnp.ones((ONES_ROWS, CHUNK), BF16)
                return carry

            lax.fori_loop(0, nc, transpose_chunk, 0)

    def body(jb, carry):
        r0, valid, dist = _causal_block_t(jb, i, zk_ref.shape[1], bk, pad)
        distf = dist.astype(F32)
        c0 = lax.div(r0, CHUNK)
        scores = []
        for g in range(H_DIFF):
            kj = zk_ref[0, pl.ds(r0, bk), g * 128:(g + 1) * 128].astype(BF16)
            nbias = jnp.where(valid, -ALIBI[g] * distf, NEG)
            scores += [_dot_nt(kj, q_sc[2 * g + m]) + nbias for m in range(2)]
        probs = []
        for a in range(2 * H_DIFF):
            m_old = m_sc[a]
            m_new = jnp.maximum(m_old, jnp.max(scores[a], axis=0, keepdims=True))
            probs.append((jnp.exp(scores[a] - m_new).astype(BF16), jnp.exp(m_old - m_new)))
            m_sc[a] = m_new
        for g in range(H_DIFF):
            vt = jnp.concatenate([vt_sc[g, c0 + c] for c in range(bk // CHUNK)], axis=1)
            for a in (2 * g, 2 * g + 1):
                acc_sc[a] = acc_sc[a] * probs[a][1] + _dot(vt, probs[a][0])
        return carry

    lax.fori_loop(0, _num_kv_blocks(i, bk), body, 0)
    for g in range(H_DIFF):
        o0 = acc_sc[2 * g]
        o1 = acc_sc[2 * g + 1]
        diff_t = (o0[:DV_DIFF] / o0[DV_DIFF:DV_DIFF + 1] - lam * (o1[:DV_DIFF] / o1[DV_DIFF:DV_DIFF + 1]))
        mix_ref[0, :, 512 + g * 128:512 + (g + 1) * 128] = (
            _rms(diff_t.T, dgn[g:g + 1]) * (1.0 - lam_init)).astype(BF16)

    @pl.when(i == nc - 1)
    def _():
        st_ref[0] = s_ref[...]


def even_prompt_mixer(z3, ret_gn, diff_gn, lam_vecs, lam_init, pad):
    B, Lp, _ = z3.shape
    nc = Lp // CHUNK
    return pl.pallas_call(
        functools.partial(_even_prompt_body, lam_init=lam_init, pad=pad, nc=nc),
        grid=(B, nc),
        in_specs=[pl.BlockSpec((1, CHUNK, 1536), lambda b, i: (b, i, 0)),
                  pl.BlockSpec((1, CHUNK, 512), lambda b, i: (b, i, 3)),
                  pl.BlockSpec((1, Lp, 512), lambda b, i: (b, 0, 4)),
                  pl.BlockSpec((1, Lp, 512), lambda b, i: (b, 0, 5)),
                  pl.BlockSpec((H_RET, DV_RET), lambda b, i: (0, 0)),
                  pl.BlockSpec((H_DIFF, DV_DIFF), lambda b, i: (0, 0)),
                  pl.BlockSpec((4, DH_DIFF), lambda b, i: (0, 0))],
        out_specs=[pl.BlockSpec((1, CHUNK, 1024), lambda b, i: (b, i, 0)),
                   pl.BlockSpec((1, 2, 128, 128), lambda b, i: (b, 0, 0, 0))],
        out_shape=[jax.ShapeDtypeStruct((B, Lp, 1024), BF16),
                   jax.ShapeDtypeStruct((B, 2, 128, 128), F32)],
        scratch_shapes=[pltpu.VMEM((2, 128, 128), F32),
                        pltpu.VMEM((2 * H_DIFF, CHUNK, 128), BF16),
                        pltpu.VMEM((2 * H_DIFF, 1, CHUNK), F32),
                        pltpu.VMEM((2 * H_DIFF, DV_DIFF + ONES_ROWS, CHUNK), F32),
                        pltpu.VMEM((H_DIFF, nc, DV_DIFF + ONES_ROWS, CHUNK), BF16)],
        compiler_params=_params(2),
        name="even_prompt_mixer",
    )(z3, z3, z3, z3, ret_gn, diff_gn, lam_vecs)


def _even_sample_body(pt_ref, zr_ref, zq_ref, zk_ref, zv_ref, st_in_ref, rgn_ref, dgn_ref, lam_ref, *rest,
                      P, n_steps, n_pages, lam_init, s_real):
    kp_refs = rest[:P]
    vp_refs = rest[P:2 * P]
    mix_ref, st_out_ref, q_sc, m_sc, l_sc, acc_sc = rest[2 * P:]
    j = pl.program_id(1)
    spad = SROWS - s_real
    row = lax.broadcasted_iota(jnp.int32, (2 * SROWS, 1), 0)
    lane = lax.broadcasted_iota(jnp.int32, (1, LANES), 1)
    rr = row & (SROWS - 1)

    @pl.when(j == 0)
    def _():
        first = lane < DH_DIFF
        for g in range(H_DIFF):
            qg = zq_ref[0, :, g * 128:(g + 1) * 128] * (DH_DIFF ** -0.5)
            q_sc[g] = jnp.concatenate([jnp.where(first, qg, 0.0), jnp.where(first, 0.0, qg)], axis=0).astype(BF16)
        m_sc[...] = jnp.full_like(m_sc, NEG)
        l_sc[...] = jnp.zeros_like(l_sc)
        acc_sc[...] = jnp.zeros_like(acc_sc)

    kcol = lax.broadcasted_iota(jnp.int32, (1, P * PAGE), 1)
    dist = ((n_pages * PAGE - spad) + rr - j * (P * PAGE) - kcol).astype(F32)
    scores = []
    for g in range(H_DIFF):
        qg = q_sc[g]
        s = jnp.concatenate(
            [_dot_nt(qg, kp_refs[r][pl.ds(g, PAGE, stride=H_DIFF), :].astype(BF16)) for r in range(P)], axis=1)
        scores.append(s - ALIBI[g] * dist)
    stats = [_softmax_step(m_sc[g], l_sc[g], scores[g]) for g in range(H_DIFF)]
    for g in range(H_DIFF):
        m_new, l_new, corr, p = stats[g]
        pb = p.astype(BF16)
        pv = None
        for r in range(P):
            t = _dot(pb[:, r * PAGE:(r + 1) * PAGE], vp_refs[r][pl.ds(g, PAGE, stride=H_DIFF), :].astype(BF16))
            pv = t if pv is None else pv + t
        m_sc[g] = m_new
        l_sc[g] = l_new
        acc_sc[g] = acc_sc[g] * corr + pv

    @pl.when(j == n_steps - 1)
    def _():
        lam = _diff_lambda(lam_ref, lam_init)
        dgn = dgn_ref[...]
        valid = jnp.logical_and(lane >= spad, lane <= rr)
        dnew = (rr - lane).astype(F32)
        for g in range(H_DIFF):
            sl = slice(g * 128, (g + 1) * 128)
            kn = _pad_rows(zk_ref[0, :, sl], PAGE).astype(BF16)
            vn = _pad_rows(zv_ref[0, :, sl], PAGE).astype(BF16)
            s = jnp.where(valid, _dot_nt(q_sc[g], kn) - ALIBI[g] * dnew, NEG)
            _, l_new, corr, p = _softmax_step(m_sc[g], l_sc[g], s)
            o = (acc_sc[g] * corr + _dot(p.astype(BF16), vn)) / l_new
            diff = o[:SROWS] - lam * o[SROWS:]
            mix_ref[0, :, 512 + g * 128:512 + (g + 1) * 128] = (
                _rms(diff, dgn[g:g + 1]) * (1.0 - lam_init)).astype(BF16)
        zr = _pad_rows(zr_ref[0], CHUNK)
        outs, new_pairs = _retention_chunk(zr, [st_in_ref[0, 0], st_in_ref[0, 1]], SROWS - s_real, s_real)
        st_out_ref[0, 0] = new_pairs[0]
        st_out_ref[0, 1] = new_pairs[1]
        ret = _retention_mix([x[:SROWS] for x in outs], zr[:SROWS], rgn_ref[...])
        for h in range(H_RET):
            mix_ref[0, :, h * 128:(h + 1) * 128] = ret[h].astype(BF16)


def even_sample_mixer(z3, cache_k, cache_v, e, page_table, st_in, ret_gn, diff_gn, lam_vecs, lam_init, s_real):
    DB = z3.shape[0]
    n_pages = page_table.shape[1]
    P = next(c for c in (32, 16, 8, 4, 2, 1) if n_pages % c == 0)
    n_steps = n_pages // P
    page_specs = [pl.BlockSpec((None, None, PAGE * H_DIFF, 128),
                               (lambda b, j, pt, r=r: (e, pt[b * n_pages + j * P + r], 0, 0)))
                  for r in range(P)]
    grid_spec = pltpu.PrefetchScalarGridSpec(
        num_scalar_prefetch=1,
        grid=(DB, n_steps),
        in_specs=[pl.BlockSpec((1, SROWS, 1536), lambda b, j, pt: (b, 0, 0)),
                  pl.BlockSpec((1, SROWS, 512), lambda b, j, pt: (b, 0, 3)),
                  pl.BlockSpec((1, SROWS, 512), lambda b, j, pt: (b, 0, 4)),
                  pl.BlockSpec((1, SROWS, 512), lambda b, j, pt: (b, 0, 5)),
                  pl.BlockSpec((1, 2, 128, 128), lambda b, j, pt: (b, 0, 0, 0)),
                  pl.BlockSpec((H_RET, DV_RET), lambda b, j, pt: (0, 0)),
                  pl.BlockSpec((H_DIFF, DV_DIFF), lambda b, j, pt: (0, 0)),
                  pl.BlockSpec((4, DH_DIFF), lambda b, j, pt: (0, 0))] + page_specs + page_specs,
        out_specs=[pl.BlockSpec((1, SROWS, 1024), lambda b, j, pt: (b, 0, 0)),
                   pl.BlockSpec((1, 2, 128, 128), lambda b, j, pt: (b, 0, 0, 0))],
        scratch_shapes=[pltpu.VMEM((H_DIFF, 2 * SROWS, 128), BF16),
                        pltpu.VMEM((H_DIFF, 2 * SROWS, 1), F32),
                        pltpu.VMEM((H_DIFF, 2 * SROWS, 1), F32),
                        pltpu.VMEM((H_DIFF, 2 * SROWS, DV_DIFF), F32)],
    )
    return pl.pallas_call(
        functools.partial(_even_sample_body, P=P, n_steps=n_steps, n_pages=n_pages, lam_init=lam_init,
                          s_real=s_real),
        grid_spec=grid_spec,
        out_shape=[jax.ShapeDtypeStruct((DB, SROWS, 1024), BF16),
                   jax.ShapeDtypeStruct((DB, 2, 128, 128), F32)],
        compiler_params=_params(2),
        name="even_sample_mixer",
    )(page_table.reshape(-1), z3, z3, z3, z3, st_in, ret_gn, diff_gn, lam_vecs,
      *([cache_k] * P), *([cache_v] * P))


def _mla_prep_prompt_body(zc_ref, zg_ref, gcq_ref, gckv_ref, wqa_ref, wqb_ref, wuk_ref, e_ref, wuv_ref,
                          cq_t_ref, sq_t_ref, ck_t_ref, sk_t_ref,
                          ckv_ref, kpe_ref, q_ref, k_ref, v_ref):
    zc = zc_ref[...]
    cq = _rms(zc[:, :D_CQ], gcq_ref[...]).astype(BF16)
    ckv = _rms(zc[:, D_CQ:], gckv_ref[...])
    ckv_ref[...] = ckv
    ckvb = ckv.astype(BF16)
    ct = jnp.concatenate([cq_t_ref[...]] * H_MLA, axis=1)
    st = jnp.concatenate([sq_t_ref[...]] * H_MLA, axis=1)
    q = (_dot(cq, wqa_ref[...]) * ct + _dot(cq, wqb_ref[...]) * st) * MLA_SCALE
    q_ref[...] = q.astype(BF16)
    zg = zg_ref[...]
    kpe = zg[:, G_KPA:G_KPA + D_ROPE] * ck_t_ref[...] + zg[:, G_KPB:G_KPB + D_ROPE] * sk_t_ref[...]
    kpe_ref[...] = kpe
    k_ref[...] = (_dot(ckvb, wuk_ref[...]) + _dot(kpe.astype(BF16), e_ref[...])).astype(BF16)
    v_ref[...] = _dot(ckvb, wuv_ref[...]).astype(BF16)


def mla_prep_prompt(z, w, tabs, tm, tiles_per_seq):
    M = z.shape[0]
    full = lambda a: pl.BlockSpec(a.shape, lambda i: (0,) * a.ndim)
    tab = lambda a: pl.BlockSpec((tm, a.shape[1]), lambda i: (i % tiles_per_seq, 0))
    weights = [w["g_cq"], w["g_ckv"], w["wqa"], w["wqb"], w["wuk_pad"], w["e_place"], w["wuv"]]
    tables = [tabs["cq"], tabs["sq"], tabs["ck"], tabs["sk"]]
    return pl.pallas_call(
        _mla_prep_prompt_body,
        grid=(M // tm,),
        in_specs=[pl.BlockSpec((tm, 512), lambda i: (i, 0)),
                  pl.BlockSpec((tm, LANES), lambda i: (i, O_GATE // LANES))]
                 + [full(a) for a in weights] + [tab(a) for a in tables],
        out_specs=[pl.BlockSpec((tm, D_CKV), lambda i: (i, 0)),
                   pl.BlockSpec((tm, D_ROPE), lambda i: (i, 0)),
                   pl.BlockSpec((tm, H_MLA * 128), lambda i: (i, 0)),
                   pl.BlockSpec((tm, H_MLA * 128), lambda i: (i, 0)),
                   pl.BlockSpec((tm, H_MLA * DV_MLA), lambda i: (i, 0))],
        out_shape=[jax.ShapeDtypeStruct((M, D_CKV), F32),
                   jax.ShapeDtypeStruct((M, D_ROPE), F32),
                   jax.ShapeDtypeStruct((M, H_MLA * 128), BF16),
                   jax.ShapeDtypeStruct((M, H_MLA * 128), BF16),
                   jax.ShapeDtypeStruct((M, H_MLA * DV_MLA), BF16)],
        compiler_params=_params(1),
        name="mla_prep_prompt",
    )(z, z, *weights, *tables)


def _mla_prep_sample_body(zc_ref, zg_ref, gcq_ref, gckv_ref, wqn_ref, wpa_ref, wpb_ref, wukt_ref,
                          cp_t_ref, sp_t_ref, ck_t_ref, sk_t_ref,
                          ckv_ref, kpe_ref, qlat_ref, qpe_ref):
    zc = zc_ref[...]
    cq = _rms(zc[:, :D_CQ], gcq_ref[...]).astype(BF16)
    ckv_ref[...] = _rms(zc[:, D_CQ:], gckv_ref[...])
    zg = zg_ref[...]
    kpe_ref[...] = zg[:, G_KPA:G_KPA + D_ROPE] * ck_t_ref[...] + zg[:, G_KPB:G_KPB + D_ROPE] * sk_t_ref[...]
    qn = _dot(cq, wqn_ref[...]) * MLA_SCALE
    col = lax.broadcasted_iota(jnp.int32, (1, H_MLA * D_NOPE), 1)
    for h in range(H_MLA):
        qh = jnp.where(jnp.right_shift(col, 6) == h, qn, 0.0).astype(BF16)
        qlat_ref[:, h * D_CKV:(h + 1) * D_CKV] = _dot(qh, wukt_ref[...]).astype(BF16)
    qpe = (_dot(cq, wpa_ref[...]) * cp_t_ref[...] + _dot(cq, wpb_ref[...]) * sp_t_ref[...]) * MLA_SCALE
    qpe_ref[...] = qpe.astype(BF16)


def mla_prep_sample(z, w, tabs):
    M = z.shape[0]
    full = lambda a: pl.BlockSpec(a.shape, lambda i: (0,) * a.ndim)
    weights = [w["g_cq"], w["g_ckv"], w["wq_nope"], w["wpe_a"], w["wpe_b"], w["wuk_t"]]
    tables = [tabs["cp"], tabs["sp"], tabs["ck"], tabs["sk"]]
    return pl.pallas_call(
        _mla_prep_sample_body,
        grid=(1,),
        in_specs=[pl.BlockSpec((M, 512), lambda i: (0, 0)),
                  pl.BlockSpec((M, LANES), lambda i: (0, O_GATE // LANES))]
                 + [full(a) for a in weights] + [full(a) for a in tables],
        out_specs=[pl.BlockSpec((M, D_CKV), lambda i: (0, 0)),
                   pl.BlockSpec((M, D_ROPE), lambda i: (0, 0)),
                   pl.BlockSpec((M, H_MLA * D_CKV), lambda i: (0, 0)),
                   pl.BlockSpec((M, H_MLA * D_ROPE), lambda i: (0, 0))],
        out_shape=[jax.ShapeDtypeStruct((M, D_CKV), F32),
                   jax.ShapeDtypeStruct((M, D_ROPE), F32),
                   jax.ShapeDtypeStruct((M, H_MLA * D_CKV), BF16),
                   jax.ShapeDtypeStruct((M, H_MLA * D_ROPE), BF16)],
        compiler_params=_params(1),
        name="mla_prep_sample",
    )(z, z, *weights, *tables)


def _store_mlstm_state(c_ref, n_ref, m_ref, cexts, ms):
    lane = lax.broadcasted_iota(jnp.int32, (1, LANES), 1)
    mrow = jnp.zeros((1, LANES), F32)
    for h in range(H_MLSTM):
        c_ref[0, h] = cexts[h][:, :DV_MLSTM]
        n_ref[0, h:h + 1, :] = cexts[h][:, DV_MLSTM:].T[0:1, :]
        mrow = jnp.where(lane == h, ms[h], mrow)
    m_ref[0] = mrow


def _odd_prompt_body(q_ref, k_ref, v_ref, mq_ref, mk_ref, mv_ref, mo_ref, gt_ref, bvec_ref, gn_ref,
                     mix_ref, c_ref, n_ref, m_ref, cext_ref, ms_ref, m_sc, acc_sc, vt_sc, *, pad, nc):
    i = pl.program_id(1)

    @pl.when(i == 0)
    def _():
        cext_ref[...] = jnp.zeros_like(cext_ref)
        ms_ref[...] = jnp.zeros_like(ms_ref)

    m_sc[...] = jnp.full_like(m_sc, NEG)
    acc_sc[...] = jnp.zeros_like(acc_sc)
    bk = k_ref.shape[1] if k_ref.shape[1] < KV_BLOCK else KV_BLOCK
    vrows = DV_MLA + ONES_ROWS

    @pl.when(i == 0)
    def _():
        for p in range(H_MLA // 2):
            def transpose_chunk(c, carry, p=p):
                r = pl.multiple_of(c * CHUNK, CHUNK)
                vt_sc[p, c, 0:ONES_ROWS, :] = jnp.ones((ONES_ROWS, CHUNK), BF16)
                vt_sc[p, c, ONES_ROWS:ONES_ROWS + LANES, :] = (
                    v_ref[0, pl.ds(r, CHUNK), p * 128:(p + 1) * 128].astype(F32).T.astype(BF16))
                vt_sc[p, c, ONES_ROWS + LANES:, :] = jnp.ones((ONES_ROWS, CHUNK), BF16)
                return carry

            lax.fori_loop(0, nc, transpose_chunk, 0)

    def body(jb, carry):
        r0, valid, _ = _causal_block_t(jb, i, k_ref.shape[1], bk, pad)
        nmask = jnp.where(valid, 0.0, NEG)
        c0 = lax.div(r0, CHUNK)
        scores = []
        for a in range(H_MLA):
            sl = slice(a * 128, (a + 1) * 128)
            scores.append(_dot_nt(k_ref[0, pl.ds(r0, bk), sl], q_ref[0, :, sl]) + nmask)
        probs = []
        for a in range(H_MLA):
            m_old = m_sc[a]
            m_new = jnp.maximum(m_old, jnp.max(scores[a], axis=0, keepdims=True))
            probs.append((jnp.exp(scores[a] - m_new).astype(BF16), jnp.exp(m_old - m_new)))
            m_sc[a] = m_new
        for p in range(H_MLA // 2):
            vt = jnp.concatenate([vt_sc[p, c0 + c] for c in range(bk // CHUNK)], axis=1)
            for s in range(2):
                a = 2 * p + s
                acc_sc[a] = acc_sc[a] * probs[a][1] + _dot(vt[s * vrows:(s + 1) * vrows], probs[a][0])
        return carry

    lax.fori_loop(0, _num_kv_blocks(i, bk), body, 0)
    for p in range(H_MLA // 2):
        oa = acc_sc[2 * p]
        ob = acc_sc[2 * p + 1]
        o_t = jnp.concatenate([oa[ONES_ROWS:] / oa[0:1], ob[:DV_MLA] / ob[DV_MLA:DV_MLA + 1]], axis=0)
        mix_ref[0, :, p * 128:(p + 1) * 128] = o_t.T.astype(BF16)

    row = lax.broadcasted_iota(jnp.int32, (CHUNK, 1), 0)
    valid = row + i * CHUNK >= pad
    cexts = [cext_ref[h] for h in range(H_MLSTM)]
    ms = [ms_ref[h:h + 1, 0:1] for h in range(H_MLSTM)]
    outs, ncs, nms = _mlstm_chunk(mq_ref[0], mk_ref[0], mv_ref[0], mo_ref[0], gt_ref[0], bvec_ref[...], valid,
                                  cexts, ms, gn_ref[...])
    for h in range(H_MLSTM):
        mix_ref[0, :, 512 + h * 128:512 + (h + 1) * 128] = outs[h].astype(BF16)
        cext_ref[h] = ncs[h]
        ms_ref[h:h + 1, :] = jnp.broadcast_to(nms[h], (1, LANES))

    @pl.when(i == nc - 1)
    def _():
        _store_mlstm_state(c_ref, n_ref, m_ref, ncs, nms)


def odd_prompt_mixer(q3, k3, v3, z3, bvec, gn, pad):
    B, Lp, _ = z3.shape
    nc = Lp // CHUNK
    zspec = lambda blk: pl.BlockSpec((1, CHUNK, 512), lambda b, i: (b, i, blk))
    return pl.pallas_call(
        functools.partial(_odd_prompt_body, pad=pad, nc=nc),
        grid=(B, nc),
        in_specs=[pl.BlockSpec((1, CHUNK, H_MLA * 128), lambda b, i: (b, i, 0)),
                  pl.BlockSpec((1, Lp, H_MLA * 128), lambda b, i: (b, 0, 0)),
                  pl.BlockSpec((1, Lp, H_MLA * DV_MLA), lambda b, i: (b, 0, 0)),
                  zspec(O_MQ // 512), zspec(O_MK // 512), zspec(O_MV // 512), zspec(O_MO // 512),
                  pl.BlockSpec((1, CHUNK, LANES), lambda b, i: (b, i, O_GATE // LANES)),
                  pl.BlockSpec((1, LANES), lambda b, i: (0, 0)),
                  pl.BlockSpec((H_MLSTM, DV_MLSTM), lambda b, i: (0, 0))],
        out_specs=[pl.BlockSpec((1, CHUNK, 1024), lambda b, i: (b, i, 0)),
                   pl.BlockSpec((1, H_MLSTM, 128, 128), lambda b, i: (b, 0, 0, 0)),
                   pl.BlockSpec((1, H_MLSTM, 128), lambda b, i: (b, 0, 0)),
                   pl.BlockSpec((1, 1, LANES), lambda b, i: (b, 0, 0))],
        out_shape=[jax.ShapeDtypeStruct((B, Lp, 1024), BF16),
                   jax.ShapeDtypeStruct((B, H_MLSTM, 128, 128), F32),
                   jax.ShapeDtypeStruct((B, H_MLSTM, 128), F32),
                   jax.ShapeDtypeStruct((B, 1, LANES), F32)],
        scratch_shapes=[pltpu.VMEM((H_MLSTM, 128, 256), F32),
                        pltpu.VMEM((8, LANES), F32),
                        pltpu.VMEM((H_MLA, 1, CHUNK), F32),
                        pltpu.VMEM((H_MLA, DV_MLA + ONES_ROWS, CHUNK), F32),
                        pltpu.VMEM((H_MLA // 2, nc, 2 * (DV_MLA + ONES_ROWS), CHUNK), BF16)],
        compiler_params=_params(2),
        name="odd_prompt_mixer",
    )(q3, k3, v3, z3, z3, z3, z3, z3, bvec, gn)


def _odd_sample_body(pt_ref, qlat_ref, qpe_ref, ckvn_ref, kpen_ref, mq_ref, mk_ref, mv_ref, mo_ref, gt_ref,
                     cin_ref, nin_ref, min_ref, bvec_ref, gn_ref, wuv_ref, *rest,
                     P, n_steps, s_real):
    cp_refs = rest[:P]
    kp_refs = rest[P:2 * P]
    mix_ref, c_ref, n_ref, m_ref, m_sc, l_sc, acc_sc = rest[2 * P:]
    j = pl.program_id(1)
    nrow = H_MLA * SROWS
    row = lax.broadcasted_iota(jnp.int32, (nrow, 1), 0)
    lane = lax.broadcasted_iota(jnp.int32, (1, LANES), 1)
    rr = row & (SROWS - 1)

    @pl.when(j == 0)
    def _():
        m_sc[...] = jnp.full_like(m_sc, NEG)
        l_sc[...] = jnp.zeros_like(l_sc)
        acc_sc[...] = jnp.zeros_like(acc_sc)

    def update(s, pages):
        m_new, l_new, corr, p = _softmax_step(m_sc[...], l_sc[...], s)
        pb = p.astype(BF16)
        pv = None
        for r, cb in enumerate(pages):
            t = _dot(pb[:, r * PAGE:(r + 1) * PAGE], cb)
            pv = t if pv is None else pv + t
        m_sc[...] = m_new
        l_sc[...] = l_new
        acc_sc[...] = acc_sc[...] * corr + pv

    qlat = qlat_ref[0]
    qpe = qpe_ref[0]
    pages = [cp_refs[r][...].astype(BF16) for r in range(P)]
    update(jnp.concatenate([_dot_nt(qlat, pages[r]) + _dot(qpe, kp_refs[r][...].astype(BF16))
                            for r in range(P)], axis=1), pages)

    @pl.when(j == n_steps - 1)
    def _():
        cn = _pad_rows(ckvn_ref[0], PAGE).astype(BF16)
        kn = _pad_rows(kpen_ref[0], PAGE).astype(BF16)
        s = _dot_nt(qlat, cn) + _dot_nt(qpe, kn)
        update(jnp.where(jnp.logical_and(lane >= SROWS - s_real, lane <= rr), s, NEG), [cn])
        o_lat = (acc_sc[...] / l_sc[...]).astype(BF16)
        r_all = _dot(o_lat, wuv_ref[...])
        col = lax.broadcasted_iota(jnp.int32, (1, H_MLA * DV_MLA), 1)
        mla = jnp.zeros((SROWS, H_MLA * DV_MLA), F32)
        for h in range(H_MLA):
            mla = jnp.where(jnp.right_shift(col, 6) == h, r_all[h * SROWS:(h + 1) * SROWS], mla)
        mix_ref[0, :, 0:512] = mla.astype(BF16)

        rowc = lax.broadcasted_iota(jnp.int32, (CHUNK, 1), 0)
        valid = jnp.logical_and(rowc >= SROWS - s_real, rowc < SROWS)
        cexts, ms = [], []
        for h in range(H_MLSTM):
            ncol = jnp.broadcast_to(nin_ref[0, h:h + 1, :], (LANES, DK_MLSTM)).T
            cexts.append(jnp.concatenate([cin_ref[0, h], ncol], axis=1))
            ms.append(min_ref[0, 0:1, h:h + 1])
        outs, ncs, nms = _mlstm_chunk(_pad_rows(mq_ref[0], CHUNK), _pad_rows(mk_ref[0], CHUNK),
                                      _pad_rows(mv_ref[0], CHUNK), _pad_rows(mo_ref[0], CHUNK),
                                      _pad_rows(gt_ref[0], CHUNK), bvec_ref[...], valid, cexts, ms, gn_ref[...])
        for h in range(H_MLSTM):
            mix_ref[0, :, 512 + h * 128:512 + (h + 1) * 128] = outs[h][:SROWS].astype(BF16)
        _store_mlstm_state(c_ref, n_ref, m_ref, ncs, nms)


def odd_sample_mixer(qlat3, qpe3, ckvn3, kpen3, z3, cache_ckv, cache_kpe, od, page_table, c_in, n_in, m_in,
                     bvec, gn, wuv, s_real):
    DB = z3.shape[0]
    n_pages = page_table.shape[1]
    P = next(c for c in (32, 16, 8, 4, 2, 1) if n_pages % c == 0)
    n_steps = n_pages // P
    nrow = H_MLA * SROWS
    pidx = lambda b, j, pt, r: (od, pt[b * n_pages + j * P + r], 0, 0)
    ckv_specs = [pl.BlockSpec((None, None, PAGE, D_CKV), functools.partial(pidx, r=r)) for r in range(P)]
    kpe_specs = [pl.BlockSpec((None, None, D_ROPE, PAGE), functools.partial(pidx, r=r)) for r in range(P)]
    zspec = lambda blk: pl.BlockSpec((1, SROWS, 512), lambda b, j, pt: (b, 0, blk))
    grid_spec = pltpu.PrefetchScalarGridSpec(
        num_scalar_prefetch=1,
        grid=(DB, n_steps),
        in_specs=[pl.BlockSpec((1, nrow, D_CKV), lambda b, j, pt: (b, 0, 0)),
                  pl.BlockSpec((1, nrow, D_ROPE), lambda b, j, pt: (b, 0, 0)),
                  pl.BlockSpec((1, SROWS, D_CKV), lambda b, j, pt: (b, 0, 0)),
                  pl.BlockSpec((1, SROWS, D_ROPE), lambda b, j, pt: (b, 0, 0)),
                  zspec(O_MQ // 512), zspec(O_MK // 512), zspec(O_MV // 512), zspec(O_MO // 512),
                  pl.BlockSpec((1, SROWS, LANES), lambda b, j, pt: (b, 0, O_GATE // LANES)),
                  pl.BlockSpec((1, H_MLSTM, 128, 128), lambda b, j, pt: (b, 0, 0, 0)),
                  pl.BlockSpec((1, H_MLSTM, 128), lambda b, j, pt: (b, 0, 0)),
                  pl.BlockSpec((1, 1, LANES), lambda b, j, pt: (b, 0, 0)),
                  pl.BlockSpec((1, LANES), lambda b, j, pt: (0, 0)),
                  pl.BlockSpec((H_MLSTM, DV_MLSTM), lambda b, j, pt: (0, 0)),
                  pl.BlockSpec((D_CKV, H_MLA * DV_MLA), lambda b, j, pt: (0, 0))] + ckv_specs + kpe_specs,
        out_specs=[pl.BlockSpec((1, SROWS, 1024), lambda b, j, pt: (b, 0, 0)),
                   pl.BlockSpec((1, H_MLSTM, 128, 128), lambda b, j, pt: (b, 0, 0, 0)),
                   pl.BlockSpec((1, H_MLSTM, 128), lambda b, j, pt: (b, 0, 0)),
                   pl.BlockSpec((1, 1, LANES), lambda b, j, pt: (b, 0, 0))],
        scratch_shapes=[pltpu.VMEM((nrow, 1), F32),
                        pltpu.VMEM((nrow, 1), F32),
                        pltpu.VMEM((nrow, D_CKV), F32)],
    )
    return pl.pallas_call(
        functools.partial(_odd_sample_body, P=P, n_steps=n_steps, s_real=s_real),
        grid_spec=grid_spec,
        out_shape=[jax.ShapeDtypeStruct((DB, SROWS, 1024), BF16),
                   jax.ShapeDtypeStruct((DB, H_MLSTM, 128, 128), F32),
                   jax.ShapeDtypeStruct((DB, H_MLSTM, 128), F32),
                   jax.ShapeDtypeStruct((DB, 1, LANES), F32)],
        compiler_params=_params(2),
        name="odd_sample_mixer",
    )(page_table.reshape(-1), qlat3, qpe3, ckvn3, kpen3, z3, z3, z3, z3, z3, c_in, n_in, m_in, bvec, gn, wuv,
      *([cache_ckv] * P), *([cache_kpe] * P))


def _odd_w_in(w):
    D = w.shape[0]
    kp1, kp2 = w[:, 512:528], w[:, 528:544]
    gate = jnp.concatenate([kp1, kp2, kp2, kp1, w[:, 2592:2596], w[:, 2596:2600],
                            jnp.zeros((D, LANES - 72), w.dtype)], axis=1)
    return jnp.concatenate([w[:, 0:512], w[:, 544:2592], gate], axis=1).astype(BF16)


def _mla_weights(g_cq, g_ckv, w_uq, w_uk, w_uv):
    C = w_uq.shape[0]
    nope, pe1, pe2 = w_uq[:, :, :D_NOPE], w_uq[:, :, D_NOPE:D_NOPE + 16], w_uq[:, :, D_NOPE + 16:]
    z64 = jnp.zeros((C, H_MLA, D_NOPE), w_uq.dtype)
    z32 = jnp.zeros((C, H_MLA, 32), w_uq.dtype)
    wqa = jnp.concatenate([nope, pe1, pe2, z32], axis=2).reshape(C, H_MLA * 128)
    wqb = jnp.concatenate([z64, pe2, pe1, z32], axis=2).reshape(C, H_MLA * 128)
    wuk_pad = jnp.concatenate([w_uk, jnp.zeros((C, H_MLA, 64), w_uk.dtype)], axis=2).reshape(C, H_MLA * 128)
    e1 = jnp.concatenate([jnp.zeros((D_ROPE, D_NOPE), F32), jnp.eye(D_ROPE, dtype=F32),
                          jnp.zeros((D_ROPE, 32), F32)], axis=1)
    return {
        "g_cq": g_cq.reshape(1, -1), "g_ckv": g_ckv.reshape(1, -1),
        "wqa": wqa.astype(BF16), "wqb": wqb.astype(BF16), "wuk_pad": wuk_pad.astype(BF16),
        "e_place": jnp.tile(e1, (1, H_MLA)).astype(BF16),
        "wuv": w_uv.reshape(C, H_MLA * DV_MLA).astype(BF16),
        "wq_nope": nope.reshape(C, H_MLA * D_NOPE).astype(BF16),
        "wpe_a": jnp.concatenate([pe1, pe2], axis=2).reshape(C, H_MLA * D_ROPE).astype(BF16),
        "wpe_b": jnp.concatenate([pe2, pe1], axis=2).reshape(C, H_MLA * D_ROPE).astype(BF16),
        "wuk_t": w_uk.transpose(1, 2, 0).reshape(H_MLA * D_NOPE, C).astype(BF16),
    }


def _rope_tables(pos):
    half = D_ROPE // 2
    freq = ROPE_THETA ** (-jnp.arange(half, dtype=F32) / half)
    ang = pos.astype(F32)[:, None] * freq[None, :]
    cos, sin = jnp.cos(ang), jnp.sin(ang)
    n = pos.shape[0]
    ck = jnp.concatenate([cos, cos], axis=1)
    sk = jnp.concatenate([-sin, sin], axis=1)
    cq = jnp.concatenate([jnp.ones((n, D_NOPE), F32), ck, jnp.zeros((n, 32), F32)], axis=1)
    sq = jnp.concatenate([jnp.zeros((n, D_NOPE), F32), sk, jnp.zeros((n, 32), F32)], axis=1)
    return {"ck": ck, "sk": sk, "cq": cq, "sq": sq,
            "cp": jnp.tile(ck, (1, H_MLA)), "sp": jnp.tile(sk, (1, H_MLA))}


def _lambda_init(layer):
    return 0.8 - 0.6 * math.exp(-0.3 * layer)


def kernel(x_prompt, x_sample, cache_diff_k, cache_diff_v, cache_mla_ckv, cache_mla_kpe, page_table, state_ret, state_mlstm_c, state_mlstm_n, state_mlstm_m, state_ffn_conv, meta_tokens, norm_mix, norm_ffn, norm_final, w_in_even, w_out_even, ret_gn, diff_gn, lam_q1, lam_k1, lam_q2, lam_k2, w_in_odd, w_out_odd, mla_g_cq, mla_g_ckv, mla_w_uq, mla_w_uk, mla_w_uv, mlstm_b_i, mlstm_b_f, mlstm_gn, ffn_w_gate, ffn_w_up, ffn_conv_w, ffn_conv_b, ffn_w_down):
    B, SEQ, D = x_prompt.shape
    DB, S, _ = x_sample.shape
    depth = norm_mix.shape[0]
    F = ffn_w_gate.shape[2]
    n_pages = page_table.shape[1]
    assert S <= SROWS
    L = SEQ + N_META
    pad = (-L) % CHUNK
    Lp = L + pad
    spad = SROWS - S
    MP, MS = B * Lp, DB * SROWS
    tmp = Lp // 2 if (Lp // 2) % 16 == 0 else Lp
    tps = Lp // tmp
    rc = Lp // 4 if Lp % 64 == 0 and Lp >= 1024 else Lp
    tf = 256

    xp = jnp.concatenate([jnp.zeros((B, pad, D), F32),
                          jnp.broadcast_to(meta_tokens[None], (B, N_META, D)), x_prompt], axis=1).reshape(MP, D)
    xs = jnp.concatenate([jnp.zeros((DB, spad, D), F32), x_sample], axis=1).reshape(MS, D)

    ck_pages = cache_diff_k.reshape(cache_diff_k.shape[0], cache_diff_k.shape[1], PAGE * H_DIFF, 2 * DH_DIFF)
    cv_pages = cache_diff_v.reshape(cache_diff_v.shape[0], cache_diff_v.shape[1], PAGE * H_DIFF, DV_DIFF)
    kpe_pages = jnp.swapaxes(cache_mla_kpe, 2, 3)

    pos_p = jnp.maximum(jnp.arange(Lp) - pad, 0)
    pos_s = jnp.tile(jnp.maximum(n_pages * PAGE + jnp.arange(SROWS) - spad, 0), DB)
    tabs_p = _rope_tables(pos_p)
    tabs_s = _rope_tables(pos_s)

    dk_p, dv_p, dk_s, dv_s, ret_p, ret_s = [], [], [], [], [], []
    ckv_p, kpe_p, ckv_s, kpe_s = [], [], [], []
    mc_p, mn_p, mm_p, mc_s, mn_s, mm_s = [], [], [], [], [], []
    cb_p, cb_s = [], []
    for layer in range(depth):
        if layer % 2 == 0:
            e = layer // 2
            li = _lambda_init(layer)
            w_in = w_in_even[e].astype(BF16)
            w_out = w_out_even[e].astype(BF16)
            lam_vecs = jnp.stack([lam_q1[e], lam_k1[e], lam_q2[e], lam_k2[e]])
            zp = norm_matmul(xp, norm_mix[layer], w_in, tmp, 512).reshape(B, Lp, -1)
            zs = norm_matmul(xs, norm_mix[layer], w_in, MS, 512).reshape(DB, SROWS, -1)
            mix_p, st_p = even_prompt_mixer(zp, ret_gn[e], diff_gn[e], lam_vecs, li, pad)
            mix_s, st_s = even_sample_mixer(zs, ck_pages, cv_pages, e, page_table,
                                            state_ret[e].reshape(DB, 2, 128, 128),
                                            ret_gn[e], diff_gn[e], lam_vecs, li, S)
            dk_p.append(zp[:, pad:, 2048:2560].reshape(B, L, H_DIFF, 2 * DH_DIFF))
            dv_p.append(zp[:, pad:, 2560:3072].reshape(B, L, H_DIFF, DV_DIFF))
            dk_s.append(zs[:, spad:, 2048:2560].reshape(DB, S, H_DIFF, 2 * DH_DIFF))
            dv_s.append(zs[:, spad:, 2560:3072].reshape(DB, S, H_DIFF, DV_DIFF))
            ret_p.append(st_p.reshape(B, H_RET, DK_RET, DV_RET))
            ret_s.append(st_s.reshape(DB, H_RET, DK_RET, DV_RET))
        else:
            od = layer // 2
            w_in = _odd_w_in(w_in_odd[od])
            w_out = w_out_odd[od].astype(BF16)
            mw = _mla_weights(mla_g_cq[od], mla_g_ckv[od], mla_w_uq[od], mla_w_uk[od], mla_w_uv[od])
            bvec = jnp.concatenate([jnp.zeros((G_MI,), F32), mlstm_b_i[od], mlstm_b_f[od],
                                    jnp.zeros((LANES - G_MF - H_MLSTM,), F32)]).reshape(1, LANES)
            zp2 = norm_matmul(xp, norm_mix[layer], w_in, tmp, 384)
            zs2 = norm_matmul(xs, norm_mix[layer], w_in, MS, 384)
            ckv_n, kpe_r, q_b, k_b, v_b = mla_prep_prompt(zp2, mw, tabs_p, tmp, tps)
            mix_p, c_p, n_p, m_p = odd_prompt_mixer(
                q_b.reshape(B, Lp, -1), k_b.reshape(B, Lp, -1), v_b.reshape(B, Lp, -1),
                zp2.reshape(B, Lp, -1), bvec, mlstm_gn[od], pad)
            ckv_ns, kpe_rs, qlat, qpe = mla_prep_sample(zs2, mw, tabs_s)
            qlat3 = qlat.reshape(DB, SROWS, H_MLA, D_CKV).transpose(0, 2, 1, 3).reshape(DB, H_MLA * SROWS, D_CKV)
            qpe3 = qpe.reshape(DB, SROWS, H_MLA, D_ROPE).transpose(0, 2, 1, 3).reshape(DB, H_MLA * SROWS, D_ROPE)
            m_in = jnp.pad(state_mlstm_m[od], ((0, 0), (0, LANES - H_MLSTM))).reshape(DB, 1, LANES)
            mix_s, c_s, n_s, m_s = odd_sample_mixer(
                qlat3, qpe3, ckv_ns.reshape(DB, SROWS, -1), kpe_rs.reshape(DB, SROWS, -1),
                zs2.reshape(DB, SROWS, -1), cache_mla_ckv, kpe_pages, od, page_table,
                state_mlstm_c[od], state_mlstm_n[od], m_in, bvec, mlstm_gn[od], mw["wuv"], S)
            ckv_p.append(ckv_n.reshape(B, Lp, -1)[:, pad:])
            kpe_p.append(kpe_r.reshape(B, Lp, -1)[:, pad:])
            ckv_s.append(ckv_ns.reshape(DB, SROWS, -1)[:, spad:])
            kpe_s.append(kpe_rs.reshape(DB, SROWS, -1)[:, spad:])
            mc_p.append(c_p)
            mn_p.append(n_p)
            mm_p.append(m_p[:, 0, :H_MLSTM])
            mc_s.append(c_s)
            mn_s.append(n_s)
            mm_s.append(m_s[:, 0, :H_MLSTM])
        xp = proj_residual(xp, mix_p.reshape(MP, -1), w_out, tmp, Lp, pad)
        xs = proj_residual(xs, mix_s.reshape(MS, -1), w_out, MS, SROWS, spad)
        wg, wu, wd = ffn_w_gate[layer].astype(BF16), ffn_w_up[layer].astype(BF16), ffn_w_down[layer].astype(BF16)
        gstate = jnp.pad(state_ffn_conv[layer], ((0, 0), (SROWS - S - (CONV_W - 1), S), (0, 0))).reshape(MS, F)
        gstate = jnp.pad(gstate, ((HALO, 0), (0, 0)))
        xp, conv_p = conv_ffn(xp, norm_ffn[layer], wg, wu, ffn_conv_w[layer], ffn_conv_b[layer], wd, Lp, rc, tf)
        xs, conv_s = conv_ffn(xs, norm_ffn[layer], wg, wu, ffn_conv_w[layer], ffn_conv_b[layer], wd, MS, MS, tf,
                              gstate=gstate)
        cb_p.append(conv_p)
        cb_s.append(conv_s.reshape(DB, SROWS, F)[:, SROWS - (CONV_W - 1):])

    off = pad + N_META
    xp3 = xp.reshape(B, Lp, D)
    if off % CHUNK == 0 and SEQ % CHUNK == 0:
        y_prompt = final_norm(xp3, norm_final, off, SEQ, CHUNK)
    else:
        y_prompt = final_norm(xp3, norm_final, 0, Lp, CHUNK)[:, off:]
    y_sample = final_norm(xs.reshape(DB, SROWS, D), norm_final, 0, SROWS, SROWS)[:, spad:]

    st = jnp.stack
    return (y_prompt, y_sample,
            st(dk_p), st(dv_p), st(dk_s), st(dv_s),
            st(ckv_p), st(kpe_p), st(ckv_s), st(kpe_s),
            st(ret_p), st(ret_s),
            st(mc_p), st(mn_p), st(mm_p), st(mc_s), st(mn_s), st(mm_s),
            st(cb_p), st(cb_s))
```

```python
import functools
import math

import jax
import jax.numpy as jnp
import numpy as np
from jax import lax
from jax.experimental import pallas as pl
from jax.experimental.pallas import tpu as pltpu

F32 = jnp.float32
BF16 = jnp.bfloat16

N_META = 16
CHUNK = 128
NORM_EPS = 1e-6
NEG = -1e30
H_RET, DK_RET, DV_RET = 4, 64, 128
H_DIFF, DH_DIFF, DV_DIFF = 4, 64, 128
H_MLA, D_NOPE, D_ROPE, DV_MLA = 8, 64, 32, 64
D_CQ, D_CKV = 256, 256
ROPE_THETA = 10000.0
H_MLSTM, DK_MLSTM, DV_MLSTM = 4, 128, 128
CONV_W = 3
PAGE = 128
SROWS = 8
LANES = 128
HALO = 16
KV_BLOCK = 512
ONES_ROWS = 16
VMEM_LIMIT = 56 * 1024 * 1024

O_CQ, O_CKV, O_MQ, O_MK, O_MV, O_MO, O_GATE = 0, 256, 512, 1024, 1536, 2048, 2560
O_IN_PAD = 2688
G_KPA, G_KPB, G_MI, G_MF = 0, 32, 64, 68

RET_LOG_GAMMA = [float(np.log(np.float32(1.0) - np.float32(2.0) ** np.float32(-5.0 - h))) for h in range(H_RET)]
ALIBI = [float(2.0 ** (-8.0 * (h + 1) / H_DIFF)) for h in range(H_DIFF)]
MLA_SCALE = float((D_NOPE + D_ROPE) ** -0.5)


def _params(n_axes, vmem=None):
    return pltpu.CompilerParams(dimension_semantics=("arbitrary",) * n_axes,
                                vmem_limit_bytes=vmem or VMEM_LIMIT)


def _dot(a, b):
    return jnp.dot(a, b, preferred_element_type=F32)


def _dot_nt(a, b):
    return lax.dot_general(a, b, (((1,), (1,)), ((), ())), preferred_element_type=F32)


def _dot_tn(a, b):
    return lax.dot_general(a, b, (((0,), (0,)), ((), ())), preferred_element_type=F32)


def _rms(x, g):
    return x * lax.rsqrt(jnp.mean(x * x, axis=-1, keepdims=True) + NORM_EPS) * g


def _layernorm(x, g):
    xc = x - jnp.mean(x, axis=-1, keepdims=True)
    return xc * lax.rsqrt(jnp.mean(xc * xc, axis=-1, keepdims=True) + NORM_EPS) * g


def _sigmoid(x):
    return 1.0 / (1.0 + jnp.exp(-x))


def _log_sigmoid(x):
    return -(jnp.maximum(-x, 0.0) + jnp.log1p(jnp.exp(-jnp.abs(x))))


def _row_valid(tm, seq_len, pad, i):
    r = lax.broadcasted_iota(jnp.int32, (tm, 1), 0)
    if tm % seq_len == 0 and seq_len & (seq_len - 1) == 0:
        pos = r & (seq_len - 1)
    else:
        assert seq_len % tm == 0
        pos = r + lax.rem(i * tm, seq_len)
    return pos >= pad


def _norm_matmul_body(x_ref, g_ref, w_ref, o_ref, h_ref):
    @pl.when(pl.program_id(1) == 0)
    def _():
        h_ref[...] = _rms(x_ref[...], g_ref[...]).astype(BF16)

    o_ref[...] = _dot(h_ref[...], w_ref[...])


def norm_matmul(x, g, w, tm, tn):
    M, K = x.shape
    N = w.shape[1]
    return pl.pallas_call(
        _norm_matmul_body,
        grid=(M // tm, N // tn),
        in_specs=[pl.BlockSpec((tm, K), lambda i, j: (i, 0)),
                  pl.BlockSpec((1, K), lambda i, j: (0, 0)),
                  pl.BlockSpec((K, tn), lambda i, j: (0, j))],
        out_specs=pl.BlockSpec((tm, tn), lambda i, j: (i, j)),
        out_shape=jax.ShapeDtypeStruct((M, N), F32),
        scratch_shapes=[pltpu.VMEM((tm, K), BF16)],
        compiler_params=_params(2),
        name="norm_matmul",
    )(x, g.reshape(1, K), w)


def _proj_res_body(x_ref, a_ref, w_ref, o_ref, *, tm, seq_len, pad):
    y = x_ref[...] + _dot(a_ref[...], w_ref[...])
    o_ref[...] = jnp.where(_row_valid(tm, seq_len, pad, pl.program_id(0)), y, 0.0)


def proj_residual(x, a, w, tm, seq_len, pad):
    M, D = x.shape
    E = a.shape[1]
    return pl.pallas_call(
        functools.partial(_proj_res_body, tm=tm, seq_len=seq_len, pad=pad),
        grid=(M // tm,),
        in_specs=[pl.BlockSpec((tm, D), lambda i: (i, 0)),
                  pl.BlockSpec((tm, E), lambda i: (i, 0)),
                  pl.BlockSpec((E, D), lambda i: (0, 0))],
        out_specs=pl.BlockSpec((tm, D), lambda i: (i, 0)),
        out_shape=jax.ShapeDtypeStruct((M, D), F32),
        compiler_params=_params(1),
        name="proj_residual",
    )(x, a, w)


def _ffn_body(*refs, tm, rc, has_state):
    if has_state:
        x_ref, gn_ref, wg_ref, wu_ref, cw_ref, cb_ref, wd_ref, gs_ref, y_ref, conv_ref, h_ref = refs
    else:
        x_ref, gn_ref, wg_ref, wu_ref, cw_ref, cb_ref, wd_ref, y_ref, conv_ref, h_ref = refs
        gs_ref = None
    j = pl.program_id(1)
    n_chunks = tm // rc
    K = x_ref.shape[-1]

    @pl.when(j == 0)
    def _():
        h_ref[pl.ds(0, HALO), :] = jnp.zeros((HALO, K), BF16)

        def norm_chunk(c, carry):
            r0 = pl.multiple_of(c * rc, HALO)
            x = x_ref[pl.ds(r0, rc), :]
            h_ref[pl.ds(r0 + HALO, rc), :] = _rms(x, gn_ref[...]).astype(BF16)
            y_ref[pl.ds(r0, rc), :] = x
            return carry

        lax.fori_loop(0, n_chunks, norm_chunk, 0)

    cw = cw_ref[...]
    cb = cb_ref[...]

    def chunk(c, carry):
        r0 = pl.multiple_of(c * rc, HALO)
        hx = h_ref[pl.ds(r0, rc + HALO), :]
        g = _dot(hx, wg_ref[...])
        if has_state:
            g = g + gs_ref[...]
        u = _dot(hx[HALO:], wu_ref[...])
        g0 = g[HALO:]
        g1 = pltpu.roll(g, 1, 0)[HALO:]
        g2 = pltpu.roll(g, 2, 0)[HALO:]
        gc = cb + cw[0:1] * g2 + cw[1:2] * g1 + cw[2:3] * g0
        a = (gc * _sigmoid(gc)) * u
        y_ref[pl.ds(r0, rc), :] += _dot(a.astype(BF16), wd_ref[...])
        if has_state:
            conv_ref[...] = g0
        return carry

    lax.fori_loop(0, n_chunks, chunk, 0)
    if not has_state:
        tail = _dot(h_ref[pl.ds(tm, HALO), :], wg_ref[...])
        conv_ref[0] = tail[HALO - (CONV_W - 1):]


def conv_ffn(x, gn, wg, wu, cw, cb, wd, tm, rc, tf, gstate=None):
    M, D = x.shape
    F = wg.shape[1]
    has_state = gstate is not None
    if has_state:
        assert rc == tm == M
    in_specs = [pl.BlockSpec((tm, D), lambda i, j: (i, 0)),
                pl.BlockSpec((1, D), lambda i, j: (0, 0)),
                pl.BlockSpec((D, tf), lambda i, j: (0, j)),
                pl.BlockSpec((D, tf), lambda i, j: (0, j)),
                pl.BlockSpec((CONV_W, tf), lambda i, j: (0, j)),
                pl.BlockSpec((1, tf), lambda i, j: (0, j)),
                pl.BlockSpec((tf, D), lambda i, j: (j, 0))]
    args = [x, gn.reshape(1, D), wg, wu, cw, cb.reshape(1, F), wd]
    if has_state:
        in_specs.append(pl.BlockSpec((tm + HALO, tf), lambda i, j: (0, j)))
        args.append(gstate)
        conv_spec = pl.BlockSpec((tm, tf), lambda i, j: (0, j))
        conv_shape = jax.ShapeDtypeStruct((M, F), F32)
    else:
        conv_spec = pl.BlockSpec((1, CONV_W - 1, tf), lambda i, j: (i, 0, j))
        conv_shape = jax.ShapeDtypeStruct((M // tm, CONV_W - 1, F), F32)
    return pl.pallas_call(
        functools.partial(_ffn_body, tm=tm, rc=rc, has_state=has_state),
        grid=(M // tm, F // tf),
        in_specs=in_specs,
        out_specs=[pl.BlockSpec((tm, D), lambda i, j: (i, 0)), conv_spec],
        out_shape=[jax.ShapeDtypeStruct((M, D), F32), conv_shape],
        scratch_shapes=[pltpu.VMEM((tm + HALO, D), BF16)],
        compiler_params=_params(2),
        name="conv_ffn",
    )(*args)


def _final_norm_body(x_ref, g_ref, o_ref):
    o_ref[0] = _rms(x_ref[0], g_ref[...])


def final_norm(x3, g, row_off, rows_out, tr):
    B, L, D = x3.shape
    assert row_off % tr == 0 and rows_out % tr == 0
    ob = row_off // tr
    return pl.pallas_call(
        _final_norm_body,
        grid=(B, rows_out // tr),
        in_specs=[pl.BlockSpec((1, tr, D), lambda b, i: (b, i + ob, 0)),
                  pl.BlockSpec((1, D), lambda b, i: (0, 0))],
        out_specs=pl.BlockSpec((1, tr, D), lambda b, i: (b, i, 0)),
        out_shape=jax.ShapeDtypeStruct((B, rows_out, D), F32),
        compiler_params=_params(2),
        name="final_norm",
    )(x3, g.reshape(1, D))


def _retention_chunk(zr, s_pairs, t_off, t_eff):
    T = zr.shape[0]
    lane = lax.broadcasted_iota(jnp.int32, (1, LANES), 1)
    first = lane < DK_RET
    ti = lax.broadcasted_iota(jnp.int32, (T, T), 0)
    si = lax.broadcasted_iota(jnp.int32, (T, T), 1)
    dist = (ti - si).astype(F32)
    tau = lax.broadcasted_iota(jnp.int32, (T, 1), 0).astype(F32) - float(t_off)
    rowi = lax.broadcasted_iota(jnp.int32, (LANES, 1), 0)
    outs, new_pairs = [], []
    for p in range(H_RET // 2):
        qp = zr[:, p * 128:(p + 1) * 128]
        kp = zr[:, 256 + p * 128:256 + (p + 1) * 128] * (DK_RET ** -0.5)
        kb = kp.astype(BF16)
        sb = s_pairs[p].astype(BF16)
        upd = None
        for s in range(2):
            h = 2 * p + s
            lg = RET_LOG_GAMMA[h]
            msk = first if s == 0 else jnp.logical_not(first)
            qh = jnp.where(msk, qp, 0.0).astype(BF16)
            decay = jnp.where(dist >= 0, jnp.exp(lg * jnp.maximum(dist, 0.0)), 0.0)
            qk = (_dot_nt(qh, kb) * decay).astype(BF16)
            vh = zr[:, 512 + h * 128:512 + (h + 1) * 128].astype(BF16)
            inner = _dot(qk, vh)
            cross = _dot(qh, sb) * jnp.exp(lg * (tau + 1.0))
            outs.append(inner + cross)
            kw = jnp.where(msk, kp * jnp.exp(lg * (float(t_eff) - 1.0 - tau)), 0.0).astype(BF16)
            u = _dot_tn(kw, vh)
            upd = u if upd is None else upd + u
        g0 = float(np.exp(np.float32(RET_LOG_GAMMA[2 * p]) * np.float32(t_eff)))
        g1 = float(np.exp(np.float32(RET_LOG_GAMMA[2 * p + 1]) * np.float32(t_eff)))
        new_pairs.append(jnp.where(rowi < DK_RET, g0, g1) * s_pairs[p] + upd)
    return outs, new_pairs


def _retention_mix(outs, zr, rgn):
    res = []
    for h in range(H_RET):
        rg = zr[:, 1024 + h * 128:1024 + (h + 1) * 128]
        gate = rg * _sigmoid(rg)
        res.append(_layernorm(outs[h], rgn[h:h + 1]) * gate)
    return res


def _softmax_step(m, l, s):
    m_new = jnp.maximum(m, jnp.max(s, axis=-1, keepdims=True))
    corr = jnp.exp(m - m_new)
    p = jnp.exp(s - m_new)
    return m_new, l * corr + jnp.sum(p, axis=-1, keepdims=True), corr, p


def _diff_lambda(lam_ref, lam_init):
    lv = lam_ref[...]
    return (jnp.exp(jnp.sum(lv[0:1] * lv[1:2], axis=-1, keepdims=True))
            - jnp.exp(jnp.sum(lv[2:3] * lv[3:4], axis=-1, keepdims=True)) + lam_init)


def _mlstm_gates(gt, bvec, valid):
    gb = gt + bvec
    li = jnp.where(valid, gb, NEG)
    lf = jnp.where(valid, _log_sigmoid(gb), 0.0)
    row = lax.broadcasted_iota(jnp.int32, (CHUNK, 1), 0)
    sh = 1
    while sh < CHUNK:
        lf = lf + jnp.where(row >= sh, pltpu.roll(lf, sh, 0), 0.0)
        sh *= 2
    lane = lax.broadcasted_iota(jnp.int32, (1, LANES), 1)
    x = jnp.where(lane >= G_MF, lf, li)
    return x, x.T


def _mlstm_head(h, x, xt, q, k, v, cext, m_prev):
    bc = x[:, G_MF + h:G_MF + h + 1]
    br = xt[G_MF + h:G_MF + h + 1, :]
    lic = x[:, G_MI + h:G_MI + h + 1]
    lir = xt[G_MI + h:G_MI + h + 1, :]
    b_last = x[CHUNK - 1:CHUNK, G_MF + h:G_MF + h + 1]
    ti = lax.broadcasted_iota(jnp.int32, (CHUNK, CHUNK), 0)
    si = lax.broadcasted_iota(jnp.int32, (CHUNK, CHUNK), 1)
    d = jnp.where(si <= ti, bc - br + lir, NEG)
    inter = bc + m_prev
    m_q = jnp.maximum(inter, jnp.max(d, axis=-1, keepdims=True))
    w = jnp.exp(d - m_q)
    gq = jnp.exp(inter - m_q)
    qb = q.astype(BF16)
    kb = k.astype(BF16)
    vb = v.astype(BF16)
    qk = _dot_nt(qb, kb) * w
    r = _dot(qb, cext.astype(BF16))
    num = _dot(qk.astype(BF16), vb) + gq * r[:, :DV_MLSTM]
    den = jnp.sum(qk, axis=-1, keepdims=True) + gq * r[:, DV_MLSTM:DV_MLSTM + 1]
    h_out = num / jnp.maximum(jnp.abs(den), jnp.exp(-m_q))
    a = b_last - bc + lic
    m_new = jnp.maximum(b_last + m_prev, jnp.max(a, axis=0, keepdims=True))
    ws = jnp.exp(a - m_new)
    gs = jnp.exp(b_last + m_prev - m_new)
    kw = (k * ws).astype(BF16)
    vext = jnp.concatenate([vb, jnp.ones((CHUNK, LANES), BF16)], axis=1)
    return h_out, gs * cext + _dot_tn(kw, vext), m_new


def _mlstm_chunk(mq, mk, mv, mo, gt, bvec, valid, cexts, ms, gn):
    x, xt = _mlstm_gates(gt, bvec, valid)
    outs, ncs, nms = [], [], []
    for h in range(H_MLSTM):
        sl = slice(h * 128, (h + 1) * 128)
        ho, nc, nm = _mlstm_head(h, x, xt, mq[:, sl], mk[:, sl] * (DK_MLSTM ** -0.5), mv[:, sl],
                                 cexts[h], ms[h])
        outs.append(_layernorm(_sigmoid(mo[:, sl]) * ho, gn[h:h + 1]))
        ncs.append(nc)
        nms.append(nm)
    return outs, ncs, nms


def _pad_rows(a, rows):
    return jnp.concatenate([a, jnp.zeros((rows - a.shape[0], a.shape[1]), a.dtype)], axis=0)


def _causal_block(jb, i, seq_rows, bk, pad):
    base = jb * bk
    r0 = pl.multiple_of(jnp.minimum(base, seq_rows - bk), CHUNK)
    kidx = r0 + lax.broadcasted_iota(jnp.int32, (1, bk), 1)
    qidx = i * CHUNK + lax.broadcasted_iota(jnp.int32, (CHUNK, 1), 0)
    dist = qidx - kidx
    valid = jnp.logical_and(dist >= 0, kidx >= jnp.maximum(base, pad))
    return r0, valid, dist


def _causal_block_t(jb, i, seq_rows, bk, pad):
    base = jb * bk
    r0 = pl.multiple_of(jnp.minimum(base, seq_rows - bk), CHUNK)
    kidx = r0 + lax.broadcasted_iota(jnp.int32, (bk, 1), 0)
    qidx = i * CHUNK + lax.broadcasted_iota(jnp.int32, (1, CHUNK), 1)
    dist = qidx - kidx
    valid = jnp.logical_and(dist >= 0, kidx >= jnp.maximum(base, pad))
    return r0, valid, dist


def _num_kv_blocks(i, bk):
    per = bk // CHUNK
    return lax.div(i + per, per)


def _even_prompt_body(zr_ref, zq_ref, zk_ref, zv_ref, rgn_ref, dgn_ref, lam_ref, mix_ref, st_ref, s_ref,
                      q_sc, m_sc, acc_sc, vt_sc, *, lam_init, pad, nc):
    i = pl.program_id(1)

    @pl.when(i == 0)
    def _():
        s_ref[...] = jnp.zeros_like(s_ref)

    zr = zr_ref[0]
    outs, new_pairs = _retention_chunk(zr, [s_ref[0], s_ref[1]], 0, CHUNK)
    s_ref[0] = new_pairs[0]
    s_ref[1] = new_pairs[1]
    ret = _retention_mix(outs, zr, rgn_ref[...])
    for h in range(H_RET):
        mix_ref[0, :, h * 128:(h + 1) * 128] = ret[h].astype(BF16)

    lam = _diff_lambda(lam_ref, lam_init)
    lane = lax.broadcasted_iota(jnp.int32, (1, LANES), 1)
    first = lane < DH_DIFF
    dgn = dgn_ref[...]
    for g in range(H_DIFF):
        qg = zq_ref[0, :, g * 128:(g + 1) * 128] * (DH_DIFF ** -0.5)
        q_sc[2 * g] = jnp.where(first, qg, 0.0).astype(BF16)
        q_sc[2 * g + 1] = jnp.where(first, 0.0, qg).astype(BF16)
    m_sc[...] = jnp.full_like(m_sc, NEG)
    acc_sc[...] = jnp.zeros_like(acc_sc)
    bk = zk_ref.shape[1] if zk_ref.shape[1] < KV_BLOCK else KV_BLOCK

    @pl.when(i == 0)
    def _():
        for g in range(H_DIFF):
            def transpose_chunk(c, carry, g=g):
                r = pl.multiple_of(c * CHUNK, CHUNK)
                vt_sc[g, c, 0:DV_DIFF, :] = zv_ref[0, pl.ds(r, CHUNK), g * 128:(g + 1) * 128].T.astype(BF16)
                vt_sc[g, c, DV_DIFF:, :] = jnp.ones((ONES_ROWS, CHUNK), BF16)
                return carry

            lax.fori_loop(0, nc, transpose_chunk, 0)

    def body(jb, carry):
        r0, valid, dist = _causal_block_t(jb, i, zk_ref.shape[1], bk, pad)
        distf = dist.astype(F32)
        c0 = lax.div(r0, CHUNK)
        scores = []
        for g in range(H_DIFF):
            kj = zk_ref[0, pl.ds(r0, bk), g * 128:(g + 1) * 128].astype(BF16)
            nbias = jnp.where(valid, -ALIBI[g] * distf, NEG)
            scores += [_dot_nt(kj, q_sc[2 * g + m]) + nbias for m in range(2)]
        probs = []
        for a in range(2 * H_DIFF):
            m_old = m_sc[a]
            m_new = jnp.maximum(m_old, jnp.max(scores[a], axis=0, keepdims=True))
            probs.append((jnp.exp(scores[a] - m_new).astype(BF16), jnp.exp(m_old - m_new)))
            m_sc[a] = m_new
        for g in range(H_DIFF):
            vt = jnp.concatenate([vt_sc[g, c0 + c] for c in range(bk // CHUNK)], axis=1)
            for a in (2 * g, 2 * g + 1):
                acc_sc[a] = acc_sc[a] * probs[a][1] + _dot(vt, probs[a][0])
        return carry

    lax.fori_loop(0, _num_kv_blocks(i, bk), body, 0)
    for g in range(H_DIFF):
        o0 = acc_sc[2 * g]
        o1 = acc_sc[2 * g + 1]
        diff_t = (o0[:DV_DIFF] / o0[DV_DIFF:DV_DIFF + 1] - lam * (o1[:DV_DIFF] / o1[DV_DIFF:DV_DIFF + 1]))
        mix_ref[0, :, 512 + g * 128:512 + (g + 1) * 128] = (
            _rms(diff_t.T, dgn[g:g + 1]) * (1.0 - lam_init)).astype(BF16)

    @pl.when(i == nc - 1)
    def _():
        st_ref[0] = s_ref[...]


def even_prompt_mixer(z3, ret_gn, diff_gn, lam_vecs, lam_init, pad):
    B, Lp, _ = z3.shape
    nc = Lp // CHUNK
    return pl.pallas_call(
        functools.partial(_even_prompt_body, lam_init=lam_init, pad=pad, nc=nc),
        grid=(B, nc),
        in_specs=[pl.BlockSpec((1, CHUNK, 1536), lambda b, i: (b, i, 0)),
                  pl.BlockSpec((1, CHUNK, 512), lambda b, i: (b, i, 3)),
                  pl.BlockSpec((1, Lp, 512), lambda b, i: (b, 0, 4)),
                  pl.BlockSpec((1, Lp, 512), lambda b, i: (b, 0, 5)),
                  pl.BlockSpec((H_RET, DV_RET), lambda b, i: (0, 0)),
                  pl.BlockSpec((H_DIFF, DV_DIFF), lambda b, i: (0, 0)),
                  pl.BlockSpec((4, DH_DIFF), lambda b, i: (0, 0))],
        out_specs=[pl.BlockSpec((1, CHUNK, 1024), lambda b, i: (b, i, 0)),
                   pl.BlockSpec((1, 2, 128, 128), lambda b, i: (b, 0, 0, 0))],
        out_shape=[jax.ShapeDtypeStruct((B, Lp, 1024), BF16),
                   jax.ShapeDtypeStruct((B, 2, 128, 128), F32)],
        scratch_shapes=[pltpu.VMEM((2, 128, 128), F32),
                        pltpu.VMEM((2 * H_DIFF, CHUNK, 128), BF16),
                        pltpu.VMEM((2 * H_DIFF, 1, CHUNK), F32),
                        pltpu.VMEM((2 * H_DIFF, DV_DIFF + ONES_ROWS, CHUNK), F32),
                        pltpu.VMEM((H_DIFF, nc, DV_DIFF + ONES_ROWS, CHUNK), BF16)],
        compiler_params=_params(2),
        name="even_prompt_mixer",
    )(z3, z3, z3, z3, ret_gn, diff_gn, lam_vecs)


def _even_sample_body(pt_ref, zr_ref, zq_ref, zk_ref, zv_ref, st_in_ref, rgn_ref, dgn_ref, lam_ref, *rest,
                      P, n_steps, n_pages, lam_init, s_real):
    kp_refs = rest[:P]
    vp_refs = rest[P:2 * P]
    mix_ref, st_out_ref, q_sc, m_sc, l_sc, acc_sc = rest[2 * P:]
    j = pl.program_id(1)
    spad = SROWS - s_real
    row = lax.broadcasted_iota(jnp.int32, (2 * SROWS, 1), 0)
    lane = lax.broadcasted_iota(jnp.int32, (1, LANES), 1)
    rr = row & (SROWS - 1)

    @pl.when(j == 0)
    def _():
        first = lane < DH_DIFF
        for g in range(H_DIFF):
            qg = zq_ref[0, :, g * 128:(g + 1) * 128] * (DH_DIFF ** -0.5)
            q_sc[g] = jnp.concatenate([jnp.where(first, qg, 0.0), jnp.where(first, 0.0, qg)], axis=0).astype(BF16)
        m_sc[...] = jnp.full_like(m_sc, NEG)
        l_sc[...] = jnp.zeros_like(l_sc)
        acc_sc[...] = jnp.zeros_like(acc_sc)

    kcol = lax.broadcasted_iota(jnp.int32, (1, P * PAGE), 1)
    dist = ((n_pages * PAGE - spad) + rr - j * (P * PAGE) - kcol).astype(F32)
    scores = []
    for g in range(H_DIFF):
        qg = q_sc[g]
        s = jnp.concatenate(
            [_dot_nt(qg, kp_refs[r][pl.ds(g, PAGE, stride=H_DIFF), :].astype(BF16)) for r in range(P)], axis=1)
        scores.append(s - ALIBI[g] * dist)
    stats = [_softmax_step(m_sc[g], l_sc[g], scores[g]) for g in range(H_DIFF)]
    for g in range(H_DIFF):
        m_new, l_new, corr, p = stats[g]
        pb = p.astype(BF16)
        pv = None
        for r in range(P):
            t = _dot(pb[:, r * PAGE:(r + 1) * PAGE], vp_refs[r][pl.ds(g, PAGE, stride=H_DIFF), :].astype(BF16))
            pv = t if pv is None else pv + t
        m_sc[g] = m_new
        l_sc[g] = l_new
        acc_sc[g] = acc_sc[g] * corr + pv

    @pl.when(j == n_steps - 1)
    def _():
        lam = _diff_lambda(lam_ref, lam_init)
        dgn = dgn_ref[...]
        valid = jnp.logical_and(lane >= spad, lane <= rr)
        dnew = (rr - lane).astype(F32)
        for g in range(H_DIFF):
            sl = slice(g * 128, (g + 1) * 128)
            kn = _pad_rows(zk_ref[0, :, sl], PAGE).astype(BF16)
            vn = _pad_rows(zv_ref[0, :, sl], PAGE).astype(BF16)
            s = jnp.where(valid, _dot_nt(q_sc[g], kn) - ALIBI[g] * dnew, NEG)
            _, l_new, corr, p = _softmax_step(m_sc[g], l_sc[g], s)
            o = (acc_sc[g] * corr + _dot(p.astype(BF16), vn)) / l_new
            diff = o[:SROWS] - lam * o[SROWS:]
            mix_ref[0, :, 512 + g * 128:512 + (g + 1) * 128] = (
                _rms(diff, dgn[g:g + 1]) * (1.0 - lam_init)).astype(BF16)
        zr = _pad_rows(zr_ref[0], CHUNK)
        outs, new_pairs = _retention_chunk(zr, [st_in_ref[0, 0], st_in_ref[0, 1]], SROWS - s_real, s_real)
        st_out_ref[0, 0] = new_pairs[0]
        st_out_ref[0, 1] = new_pairs[1]
        ret = _retention_mix([x[:SROWS] for x in outs], zr[:SROWS], rgn_ref[...])
        for h in range(H_RET):
            mix_ref[0, :, h * 128:(h + 1) * 128] = ret[h].astype(BF16)


def even_sample_mixer(z3, cache_k, cache_v, e, page_table, st_in, ret_gn, diff_gn, lam_vecs, lam_init, s_real):
    DB = z3.shape[0]
    n_pages = page_table.shape[1]
    P = next(c for c in (32, 16, 8, 4, 2, 1) if n_pages % c == 0)
    n_steps = n_pages // P
    page_specs = [pl.BlockSpec((None, None, PAGE * H_DIFF, 128),
                               (lambda b, j, pt, r=r: (e, pt[b * n_pages + j * P + r], 0, 0)))
                  for r in range(P)]
    grid_spec = pltpu.PrefetchScalarGridSpec(
        num_scalar_prefetch=1,
        grid=(DB, n_steps),
        in_specs=[pl.BlockSpec((1, SROWS, 1536), lambda b, j, pt: (b, 0, 0)),
                  pl.BlockSpec((1, SROWS, 512), lambda b, j, pt: (b, 0, 3)),
                  pl.BlockSpec((1, SROWS, 512), lambda b, j, pt: (b, 0, 4)),
                  pl.BlockSpec((1, SROWS, 512), lambda b, j, pt: (b, 0, 5)),
                  pl.BlockSpec((1, 2, 128, 128), lambda b, j, pt: (b, 0, 0, 0)),
                  pl.BlockSpec((H_RET, DV_RET), lambda b, j, pt: (0, 0)),
                  pl.BlockSpec((H_DIFF, DV_DIFF), lambda b, j, pt: (0, 0)),
                  pl.BlockSpec((4, DH_DIFF), lambda b, j, pt: (0, 0))] + page_specs + page_specs,
        out_specs=[pl.BlockSpec((1, SROWS, 1024), lambda b, j, pt: (b, 0, 0)),
                   pl.BlockSpec((1, 2, 128, 128), lambda b, j, pt: (b, 0, 0, 0))],
        scratch_shapes=[pltpu.VMEM((H_DIFF, 2 * SROWS, 128), BF16),
                        pltpu.VMEM((H_DIFF, 2 * SROWS, 1), F32),
                        pltpu.VMEM((H_DIFF, 2 * SROWS, 1), F32),
                        pltpu.VMEM((H_DIFF, 2 * SROWS, DV_DIFF), F32)],
    )
    return pl.pallas_call(
        functools.partial(_even_sample_body, P=P, n_steps=n_steps, n_pages=n_pages, lam_init=lam_init,
                          s_real=s_real),
        grid_spec=grid_spec,
        out_shape=[jax.ShapeDtypeStruct((DB, SROWS, 1024), BF16),
                   jax.ShapeDtypeStruct((DB, 2, 128, 128), F32)],
        compiler_params=_params(2),
        name="even_sample_mixer",
    )(page_table.reshape(-1), z3, z3, z3, z3, st_in, ret_gn, diff_gn, lam_vecs,
      *([cache_k] * P), *([cache_v] * P))


def _mla_prep_prompt_body(zc_ref, zg_ref, gcq_ref, gckv_ref, wqa_ref, wqb_ref, wuk_ref, e_ref, wuv_ref,
                          cq_t_ref, sq_t_ref, ck_t_ref, sk_t_ref,
                          ckv_ref, kpe_ref, q_ref, k_ref, v_ref):
    zc = zc_ref[...]
    cq = _rms(zc[:, :D_CQ], gcq_ref[...]).astype(BF16)
    ckv = _rms(zc[:, D_CQ:], gckv_ref[...])
    ckv_ref[...] = ckv
    ckvb = ckv.astype(BF16)
    ct = jnp.concatenate([cq_t_ref[...]] * H_MLA, axis=1)
    st = jnp.concatenate([sq_t_ref[...]] * H_MLA, axis=1)
    q = (_dot(cq, wqa_ref[...]) * ct + _dot(cq, wqb_ref[...]) * st) * MLA_SCALE
    q_ref[...] = q.astype(BF16)
    zg = zg_ref[...]
    kpe = zg[:, G_KPA:G_KPA + D_ROPE] * ck_t_ref[...] + zg[:, G_KPB:G_KPB + D_ROPE] * sk_t_ref[...]
    kpe_ref[...] = kpe
    k_ref[...] = (_dot(ckvb, wuk_ref[...]) + _dot(kpe.astype(BF16), e_ref[...])).astype(BF16)
    v_ref[...] = _dot(ckvb, wuv_ref[...]).astype(BF16)


def mla_prep_prompt(z, w, tabs, tm, tiles_per_seq):
    M = z.shape[0]
    full = lambda a: pl.BlockSpec(a.shape, lambda i: (0,) * a.ndim)
    tab = lambda a: pl.BlockSpec((tm, a.shape[1]), lambda i: (i % tiles_per_seq, 0))
    weights = [w["g_cq"], w["g_ckv"], w["wqa"], w["wqb"], w["wuk_pad"], w["e_place"], w["wuv"]]
    tables = [tabs["cq"], tabs["sq"], tabs["ck"], tabs["sk"]]
    return pl.pallas_call(
        _mla_prep_prompt_body,
        grid=(M // tm,),
        in_specs=[pl.BlockSpec((tm, 512), lambda i: (i, 0)),
                  pl.BlockSpec((tm, LANES), lambda i: (i, O_GATE // LANES))]
                 + [full(a) for a in weights] + [tab(a) for a in tables],
        out_specs=[pl.BlockSpec((tm, D_CKV), lambda i: (i, 0)),
                   pl.BlockSpec((tm, D_ROPE), lambda i: (i, 0)),
                   pl.BlockSpec((tm, H_MLA * 128), lambda i: (i, 0)),
                   pl.BlockSpec((tm, H_MLA * 128), lambda i: (i, 0)),
                   pl.BlockSpec((tm, H_MLA * DV_MLA), lambda i: (i, 0))],
        out_shape=[jax.ShapeDtypeStruct((M, D_CKV), F32),
                   jax.ShapeDtypeStruct((M, D_ROPE), F32),
                   jax.ShapeDtypeStruct((M, H_MLA * 128), BF16),
                   jax.ShapeDtypeStruct((M, H_MLA * 128), BF16),
                   jax.ShapeDtypeStruct((M, H_MLA * DV_MLA), BF16)],
        compiler_params=_params(1),
        name="mla_prep_prompt",
    )(z, z, *weights, *tables)


def _mla_prep_sample_body(zc_ref, zg_ref, gcq_ref, gckv_ref, wqn_ref, wpa_ref, wpb_ref, wukt_ref,
                          cp_t_ref, sp_t_ref, ck_t_ref, sk_t_ref,
                          ckv_ref, kpe_ref, qlat_ref, qpe_ref):
    zc = zc_ref[...]
    cq = _rms(zc[:, :D_CQ], gcq_ref[...]).astype(BF16)
    ckv_ref[...] = _rms(zc[:, D_CQ:], gckv_ref[...])
    zg = zg_ref[...]
    kpe_ref[...] = zg[:, G_KPA:G_KPA + D_ROPE] * ck_t_ref[...] + zg[:, G_KPB:G_KPB + D_ROPE] * sk_t_ref[...]
    qn = _dot(cq, wqn_ref[...]) * MLA_SCALE
    col = lax.broadcasted_iota(jnp.int32, (1, H_MLA * D_NOPE), 1)
    for h in range(H_MLA):
        qh = jnp.where(jnp.right_shift(col, 6) == h, qn, 0.0).astype(BF16)
        qlat_ref[:, h * D_CKV:(h + 1) * D_CKV] = _dot(qh, wukt_ref[...]).astype(BF16)
    qpe = (_dot(cq, wpa_ref[...]) * cp_t_ref[...] + _dot(cq, wpb_ref[...]) * sp_t_ref[...]) * MLA_SCALE
    qpe_ref[...] = qpe.astype(BF16)


def mla_prep_sample(z, w, tabs):
    M = z.shape[0]
    full = lambda a: pl.BlockSpec(a.shape, lambda i: (0,) * a.ndim)
    weights = [w["g_cq"], w["g_ckv"], w["wq_nope"], w["wpe_a"], w["wpe_b"], w["wuk_t"]]
    tables = [tabs["cp"], tabs["sp"], tabs["ck"], tabs["sk"]]
    return pl.pallas_call(
        _mla_prep_sample_body,
        grid=(1,),
        in_specs=[pl.BlockSpec((M, 512), lambda i: (0, 0)),
                  pl.BlockSpec((M, LANES), lambda i: (0, O_GATE // LANES))]
                 + [full(a) for a in weights] + [full(a) for a in tables],
        out_specs=[pl.BlockSpec((M, D_CKV), lambda i: (0, 0)),
                   pl.BlockSpec((M, D_ROPE), lambda i: (0, 0)),
                   pl.BlockSpec((M, H_MLA * D_CKV), lambda i: (0, 0)),
                   pl.BlockSpec((M, H_MLA * D_ROPE), lambda i: (0, 0))],
        out_shape=[jax.ShapeDtypeStruct((M, D_CKV), F32),
                   jax.ShapeDtypeStruct((M, D_ROPE), F32),
                   jax.ShapeDtypeStruct((M, H_MLA * D_CKV), BF16),
                   jax.ShapeDtypeStruct((M, H_MLA * D_ROPE), BF16)],
        compiler_params=_params(1),
        name="mla_prep_sample",
    )(z, z, *weights, *tables)


def _store_mlstm_state(c_ref, n_ref, m_ref, cexts, ms):
    lane = lax.broadcasted_iota(jnp.int32, (1, LANES), 1)
    mrow = jnp.zeros((1, LANES), F32)
    for h in range(H_MLSTM):
        c_ref[0, h] = cexts[h][:, :DV_MLSTM]
        n_ref[0, h:h + 1, :] = cexts[h][:, DV_MLSTM:].T[0:1, :]
        mrow = jnp.where(lane == h, ms[h], mrow)
    m_ref[0] = mrow


def _odd_prompt_body(q_ref, k_ref, v_ref, mq_ref, mk_ref, mv_ref, mo_ref, gt_ref, bvec_ref, gn_ref,
                     mix_ref, c_ref, n_ref, m_ref, cext_ref, ms_ref, m_sc, acc_sc, vt_sc, *, pad, nc):
    i = pl.program_id(1)

    @pl.when(i == 0)
    def _():
        cext_ref[...] = jnp.zeros_like(cext_ref)
        ms_ref[...] = jnp.zeros_like(ms_ref)

    m_sc[...] = jnp.full_like(m_sc, NEG)
    acc_sc[...] = jnp.zeros_like(acc_sc)
    bk = k_ref.shape[1] if k_ref.shape[1] < KV_BLOCK else KV_BLOCK
    vrows = DV_MLA + ONES_ROWS

    @pl.when(i == 0)
    def _():
        for p in range(H_MLA // 2):
            def transpose_chunk(c, carry, p=p):
                r = pl.multiple_of(c * CHUNK, CHUNK)
                vt_sc[p, c, 0:ONES_ROWS, :] = jnp.ones((ONES_ROWS, CHUNK), BF16)
                vt_sc[p, c, ONES_ROWS:ONES_ROWS + LANES, :] = (
                    v_ref[0, pl.ds(r, CHUNK), p * 128:(p + 1) * 128].astype(F32).T.astype(BF16))
                vt_sc[p, c, ONES_ROWS + LANES:, :] = jnp.ones((ONES_ROWS, CHUNK), BF16)
                return carry

            lax.fori_loop(0, nc, transpose_chunk, 0)

    def body(jb, carry):
        r0, valid, _ = _causal_block_t(jb, i, k_ref.shape[1], bk, pad)
        nmask = jnp.where(valid, 0.0, NEG)
        c0 = lax.div(r0, CHUNK)
        scores = []
        for a in range(H_MLA):
            sl = slice(a * 128, (a + 1) * 128)
            scores.append(_dot_nt(k_ref[0, pl.ds(r0, bk), sl], q_ref[0, :, sl]) + nmask)
        probs = []
        for a in range(H_MLA):
            m_old = m_sc[a]
            m_new = jnp.maximum(m_old, jnp.max(scores[a], axis=0, keepdims=True))
            probs.append((jnp.exp(scores[a] - m_new).astype(BF16), jnp.exp(m_old - m_new)))
            m_sc[a] = m_new
        for p in range(H_MLA // 2):
            vt = jnp.concatenate([vt_sc[p, c0 + c] for c in range(bk // CHUNK)], axis=1)
            for s in range(2):
                a = 2 * p + s
                acc_sc[a] = acc_sc[a] * probs[a][1] + _dot(vt[s * vrows:(s + 1) * vrows], probs[a][0])
        return carry

    lax.fori_loop(0, _num_kv_blocks(i, bk), body, 0)
    for p in range(H_MLA // 2):
        oa = acc_sc[2 * p]
        ob = acc_sc[2 * p + 1]
        o_t = jnp.concatenate([oa[ONES_ROWS:] / oa[0:1], ob[:DV_MLA] / ob[DV_MLA:DV_MLA + 1]], axis=0)
        mix_ref[0, :, p * 128:(p + 1) * 128] = o_t.T.astype(BF16)

    row = lax.broadcasted_iota(jnp.int32, (CHUNK, 1), 0)
    valid = row + i * CHUNK >= pad
    cexts = [cext_ref[h] for h in range(H_MLSTM)]
    ms = [ms_ref[h:h + 1, 0:1] for h in range(H_MLSTM)]
    outs, ncs, nms = _mlstm_chunk(mq_ref[0], mk_ref[0], mv_ref[0], mo_ref[0], gt_ref[0], bvec_ref[...], valid,
                                  cexts, ms, gn_ref[...])
    for h in range(H_MLSTM):
        mix_ref[0, :, 512 + h * 128:512 + (h + 1) * 128] = outs[h].astype(BF16)
        cext_ref[h] = ncs[h]
        ms_ref[h:h + 1, :] = jnp.broadcast_to(nms[h], (1, LANES))

    @pl.when(i == nc - 1)
    def _():
        _store_mlstm_state(c_ref, n_ref, m_ref, ncs, nms)


def odd_prompt_mixer(q3, k3, v3, z3, bvec, gn, pad):
    B, Lp, _ = z3.shape
    nc = Lp // CHUNK
    zspec = lambda blk: pl.BlockSpec((1, CHUNK, 512), lambda b, i: (b, i, blk))
    return pl.pallas_call(
        functools.partial(_odd_prompt_body, pad=pad, nc=nc),
        grid=(B, nc),
        in_specs=[pl.BlockSpec((1, CHUNK, H_MLA * 128), lambda b, i: (b, i, 0)),
                  pl.BlockSpec((1, Lp, H_MLA * 128), lambda b, i: (b, 0, 0)),
                  pl.BlockSpec((1, Lp, H_MLA * DV_MLA), lambda b, i: (b, 0, 0)),
                  zspec(O_MQ // 512), zspec(O_MK // 512), zspec(O_MV // 512), zspec(O_MO // 512),
                  pl.BlockSpec((1, CHUNK, LANES), lambda b, i: (b, i, O_GATE // LANES)),
                  pl.BlockSpec((1, LANES), lambda b, i: (0, 0)),
                  pl.BlockSpec((H_MLSTM, DV_MLSTM), lambda b, i: (0, 0))],
        out_specs=[pl.BlockSpec((1, CHUNK, 1024), lambda b, i: (b, i, 0)),
                   pl.BlockSpec((1, H_MLSTM, 128, 128), lambda b, i: (b, 0, 0, 0)),
                   pl.BlockSpec((1, H_MLSTM, 128), lambda b, i: (b, 0, 0)),
                   pl.BlockSpec((1, 1, LANES), lambda b, i: (b, 0, 0))],
        out_shape=[jax.ShapeDtypeStruct((B, Lp, 1024), BF16),
                   jax.ShapeDtypeStruct((B, H_MLSTM, 128, 128), F32),
                   jax.ShapeDtypeStruct((B, H_MLSTM, 128), F32),
                   jax.ShapeDtypeStruct((B, 1, LANES), F32)],
        scratch_shapes=[pltpu.VMEM((H_MLSTM, 128, 256), F32),
                        pltpu.VMEM((8, LANES), F32),
                        pltpu.VMEM((H_MLA, 1, CHUNK), F32),
                        pltpu.VMEM((H_MLA, DV_MLA + ONES_ROWS, CHUNK), F32),
                        pltpu.VMEM((H_MLA // 2, nc, 2 * (DV_MLA + ONES_ROWS), CHUNK), BF16)],
        compiler_params=_params(2),
        name="odd_prompt_mixer",
    )(q3, k3, v3, z3, z3, z3, z3, z3, bvec, gn)


def _odd_sample_body(pt_ref, qlat_ref, qpe_ref, ckvn_ref, kpen_ref, mq_ref, mk_ref, mv_ref, mo_ref, gt_ref,
                     cin_ref, nin_ref, min_ref, bvec_ref, gn_ref, wuv_ref, *rest,
                     P, n_steps, s_real):
    cp_refs = rest[:P]
    kp_refs = rest[P:2 * P]
    mix_ref, c_ref, n_ref, m_ref, m_sc, l_sc, acc_sc = rest[2 * P:]
    j = pl.program_id(1)
    nrow = H_MLA * SROWS
    row = lax.broadcasted_iota(jnp.int32, (nrow, 1), 0)
    lane = lax.broadcasted_iota(jnp.int32, (1, LANES), 1)
    rr = row & (SROWS - 1)

    @pl.when(j == 0)
    def _():
        m_sc[...] = jnp.full_like(m_sc, NEG)
        l_sc[...] = jnp.zeros_like(l_sc)
        acc_sc[...] = jnp.zeros_like(acc_sc)

    def update(streams):
        stats = [_softmax_step(m_sc[t], l_sc[t], s) for t, s, _ in streams]
        for (t, _, pages), (m_new, l_new, corr, p) in zip(streams, stats):
            pb = p.astype(BF16)
            pv = None
            for r, cb in enumerate(pages):
                u = _dot(pb[:, r * PAGE:(r + 1) * PAGE], cb)
                pv = u if pv is None else pv + u
            m_sc[t] = m_new
            l_sc[t] = l_new
            acc_sc[t] = acc_sc[t] * corr + pv

    qlat = qlat_ref[0]
    qpe = qpe_ref[0]
    ns = m_sc.shape[0]
    per = P // ns
    pages = [cp_refs[r][...].astype(BF16) for r in range(P)]
    scores = [_dot_nt(qlat, pages[r]) + _dot(qpe, kp_refs[r][...].astype(BF16)) for r in range(P)]
    update([(t, jnp.concatenate(scores[t * per:(t + 1) * per], axis=1), pages[t * per:(t + 1) * per])
            for t in range(ns)])

    @pl.when(j == n_steps - 1)
    def _():
        cn = _pad_rows(ckvn_ref[0], PAGE).astype(BF16)
        kn = _pad_rows(kpen_ref[0], PAGE).astype(BF16)
        s = _dot_nt(qlat, cn) + _dot_nt(qpe, kn)
        update([(0, jnp.where(jnp.logical_and(lane >= SROWS - s_real, lane <= rr), s, NEG), [cn])])
        m_all, l_all, acc_all = m_sc[0], l_sc[0], acc_sc[0]
        for t in range(1, ns):
            m_new = jnp.maximum(m_all, m_sc[t])
            ca = jnp.exp(m_all - m_new)
            cb = jnp.exp(m_sc[t] - m_new)
            l_all = l_all * ca + l_sc[t] * cb
            acc_all = acc_all * ca + acc_sc[t] * cb
            m_all = m_new
        o_lat = (acc_all / l_all).astype(BF16)
        r_all = _dot(o_lat, wuv_ref[...])
        col = lax.broadcasted_iota(jnp.int32, (1, H_MLA * DV_MLA), 1)
        mla = jnp.zeros((SROWS, H_MLA * DV_MLA), F32)
        for h in range(H_MLA):
            mla = jnp.where(jnp.right_shift(col, 6) == h, r_all[h * SROWS:(h + 1) * SROWS], mla)
        mix_ref[0, :, 0:512] = mla.astype(BF16)

        rowc = lax.broadcasted_iota(jnp.int32, (CHUNK, 1), 0)
        valid = jnp.logical_and(rowc >= SROWS - s_real, rowc < SROWS)
        cexts, ms = [], []
        for h in range(H_MLSTM):
            ncol = jnp.broadcast_to(nin_ref[0, h:h + 1, :], (LANES, DK_MLSTM)).T
            cexts.append(jnp.concatenate([cin_ref[0, h], ncol], axis=1))
            ms.append(min_ref[0, 0:1, h:h + 1])
        outs, ncs, nms = _mlstm_chunk(_pad_rows(mq_ref[0], CHUNK), _pad_rows(mk_ref[0], CHUNK),
                                      _pad_rows(mv_ref[0], CHUNK), _pad_rows(mo_ref[0], CHUNK),
                                      _pad_rows(gt_ref[0], CHUNK), bvec_ref[...], valid, cexts, ms, gn_ref[...])
        for h in range(H_MLSTM):
            mix_ref[0, :, 512 + h * 128:512 + (h + 1) * 128] = outs[h][:SROWS].astype(BF16)
        _store_mlstm_state(c_ref, n_ref, m_ref, ncs, nms)


def odd_sample_mixer(qlat3, qpe3, ckvn3, kpen3, z3, cache_ckv, cache_kpe, od, page_table, c_in, n_in, m_in,
                     bvec, gn, wuv, s_real):
    DB = z3.shape[0]
    n_pages = page_table.shape[1]
    P = next(c for c in (32, 16, 8, 4, 2, 1) if n_pages % c == 0)
    n_steps = n_pages // P
    ns = 2 if P % 2 == 0 else 1
    nrow = H_MLA * SROWS
    pidx = lambda b, j, pt, r: (od, pt[b * n_pages + j * P + r], 0, 0)
    ckv_specs = [pl.BlockSpec((None, None, PAGE, D_CKV), functools.partial(pidx, r=r)) for r in range(P)]
    kpe_specs = [pl.BlockSpec((None, None, D_ROPE, PAGE), functools.partial(pidx, r=r)) for r in range(P)]
    zspec = lambda blk: pl.BlockSpec((1, SROWS, 512), lambda b, j, pt: (b, 0, blk))
    grid_spec = pltpu.PrefetchScalarGridSpec(
        num_scalar_prefetch=1,
        grid=(DB, n_steps),
        in_specs=[pl.BlockSpec((1, nrow, D_CKV), lambda b, j, pt: (b, 0, 0)),
                  pl.BlockSpec((1, nrow, D_ROPE), lambda b, j, pt: (b, 0, 0)),
                  pl.BlockSpec((1, SROWS, D_CKV), lambda b, j, pt: (b, 0, 0)),
                  pl.BlockSpec((1, SROWS, D_ROPE), lambda b, j, pt: (b, 0, 0)),
                  zspec(O_MQ // 512), zspec(O_MK // 512), zspec(O_MV // 512), zspec(O_MO // 512),
                  pl.BlockSpec((1, SROWS, LANES), lambda b, j, pt: (b, 0, O_GATE // LANES)),
                  pl.BlockSpec((1, H_MLSTM, 128, 128), lambda b, j, pt: (b, 0, 0, 0)),
                  pl.BlockSpec((1, H_MLSTM, 128), lambda b, j, pt: (b, 0, 0)),
                  pl.BlockSpec((1, 1, LANES), lambda b, j, pt: (b, 0, 0)),
                  pl.BlockSpec((1, LANES), lambda b, j, pt: (0, 0)),
                  pl.BlockSpec((H_MLSTM, DV_MLSTM), lambda b, j, pt: (0, 0)),
                  pl.BlockSpec((D_CKV, H_MLA * DV_MLA), lambda b, j, pt: (0, 0))] + ckv_specs + kpe_specs,
        out_specs=[pl.BlockSpec((1, SROWS, 1024), lambda b, j, pt: (b, 0, 0)),
                   pl.BlockSpec((1, H_MLSTM, 128, 128), lambda b, j, pt: (b, 0, 0, 0)),
                   pl.BlockSpec((1, H_MLSTM, 128), lambda b, j, pt: (b, 0, 0)),
                   pl.BlockSpec((1, 1, LANES), lambda b, j, pt: (b, 0, 0))],
        scratch_shapes=[pltpu.VMEM((ns, nrow, 1), F32),
                        pltpu.VMEM((ns, nrow, 1), F32),
                        pltpu.VMEM((ns, nrow, D_CKV), F32)],
    )
    return pl.pallas_call(
        functools.partial(_odd_sample_body, P=P, n_steps=n_steps, s_real=s_real),
        grid_spec=grid_spec,
        out_shape=[jax.ShapeDtypeStruct((DB, SROWS, 1024), BF16),
                   jax.ShapeDtypeStruct((DB, H_MLSTM, 128, 128), F32),
                   jax.ShapeDtypeStruct((DB, H_MLSTM, 128), F32),
                   jax.ShapeDtypeStruct((DB, 1, LANES), F32)],
        compiler_params=_params(2),
        name="odd_sample_mixer",
    )(page_table.reshape(-1), qlat3, qpe3, ckvn3, kpen3, z3, z3, z3, z3, z3, c_in, n_in, m_in, bvec, gn, wuv,
      *([cache_ckv] * P), *([cache_kpe] * P))


def _odd_w_in(w):
    D = w.shape[0]
    kp1, kp2 = w[:, 512:528], w[:, 528:544]
    gate = jnp.concatenate([kp1, kp2, kp2, kp1, w[:, 2592:2596], w[:, 2596:2600],
                            jnp.zeros((D, LANES - 72), w.dtype)], axis=1)
    return jnp.concatenate([w[:, 0:512], w[:, 544:2592], gate], axis=1).astype(BF16)


def _mla_weights(g_cq, g_ckv, w_uq, w_uk, w_uv):
    C = w_uq.shape[0]
    nope, pe1, pe2 = w_uq[:, :, :D_NOPE], w_uq[:, :, D_NOPE:D_NOPE + 16], w_uq[:, :, D_NOPE + 16:]
    z64 = jnp.zeros((C, H_MLA, D_NOPE), w_uq.dtype)
    z32 = jnp.zeros((C, H_MLA, 32), w_uq.dtype)
    wqa = jnp.concatenate([nope, pe1, pe2, z32], axis=2).reshape(C, H_MLA * 128)
    wqb = jnp.concatenate([z64, pe2, pe1, z32], axis=2).reshape(C, H_MLA * 128)
    wuk_pad = jnp.concatenate([w_uk, jnp.zeros((C, H_MLA, 64), w_uk.dtype)], axis=2).reshape(C, H_MLA * 128)
    e1 = jnp.concatenate([jnp.zeros((D_ROPE, D_NOPE), F32), jnp.eye(D_ROPE, dtype=F32),
                          jnp.zeros((D_ROPE, 32), F32)], axis=1)
    return {
        "g_cq": g_cq.reshape(1, -1), "g_ckv": g_ckv.reshape(1, -1),
        "wqa": wqa.astype(BF16), "wqb": wqb.astype(BF16), "wuk_pad": wuk_pad.astype(BF16),
        "e_place": jnp.tile(e1, (1, H_MLA)).astype(BF16),
        "wuv": w_uv.reshape(C, H_MLA * DV_MLA).astype(BF16),
        "wq_nope": nope.reshape(C, H_MLA * D_NOPE).astype(BF16),
        "wpe_a": jnp.concatenate([pe1, pe2], axis=2).reshape(C, H_MLA * D_ROPE).astype(BF16),
        "wpe_b": jnp.concatenate([pe2, pe1], axis=2).reshape(C, H_MLA * D_ROPE).astype(BF16),
        "wuk_t": w_uk.transpose(1, 2, 0).reshape(H_MLA * D_NOPE, C).astype(BF16),
    }


def _rope_tables(pos):
    half = D_ROPE // 2
    freq = ROPE_THETA ** (-jnp.arange(half, dtype=F32) / half)
    ang = pos.astype(F32)[:, None] * freq[None, :]
    cos, sin = jnp.cos(ang), jnp.sin(ang)
    n = pos.shape[0]
    ck = jnp.concatenate([cos, cos], axis=1)
    sk = jnp.concatenate([-sin, sin], axis=1)
    cq = jnp.concatenate([jnp.ones((n, D_NOPE), F32), ck, jnp.zeros((n, 32), F32)], axis=1)
    sq = jnp.concatenate([jnp.zeros((n, D_NOPE), F32), sk, jnp.zeros((n, 32), F32)], axis=1)
    return {"ck": ck, "sk": sk, "cq": cq, "sq": sq,
            "cp": jnp.tile(ck, (1, H_MLA)), "sp": jnp.tile(sk, (1, H_MLA))}


def _lambda_init(layer):
    return 0.8 - 0.6 * math.exp(-0.3 * layer)


def kernel(x_prompt, x_sample, cache_diff_k, cache_diff_v, cache_mla_ckv, cache_mla_kpe, page_table, state_ret, state_mlstm_c, state_mlstm_n, state_mlstm_m, state_ffn_conv, meta_tokens, norm_mix, norm_ffn, norm_final, w_in_even, w_out_even, ret_gn, diff_gn, lam_q1, lam_k1, lam_q2, lam_k2, w_in_odd, w_out_odd, mla_g_cq, mla_g_ckv, mla_w_uq, mla_w_uk, mla_w_uv, mlstm_b_i, mlstm_b_f, mlstm_gn, ffn_w_gate, ffn_w_up, ffn_conv_w, ffn_conv_b, ffn_w_down):
    B, SEQ, D = x_prompt.shape
    DB, S, _ = x_sample.shape
    depth = norm_mix.shape[0]
    F = ffn_w_gate.shape[2]
    n_pages = page_table.shape[1]
    assert S <= SROWS
    L = SEQ + N_META
    pad = (-L) % CHUNK
    Lp = L + pad
    spad = SROWS - S
    MP, MS = B * Lp, DB * SROWS
    tmp = Lp // 2 if (Lp // 2) % 16 == 0 else Lp
    tps = Lp // tmp
    rc = Lp // 4 if Lp % 64 == 0 and Lp >= 1024 else Lp
    tf = 256

    xp = jnp.concatenate([jnp.zeros((B, pad, D), F32),
                          jnp.broadcast_to(meta_tokens[None], (B, N_META, D)), x_prompt], axis=1).reshape(MP, D)
    xs = jnp.concatenate([jnp.zeros((DB, spad, D), F32), x_sample], axis=1).reshape(MS, D)

    ck_pages = cache_diff_k.reshape(cache_diff_k.shape[0], cache_diff_k.shape[1], PAGE * H_DIFF, 2 * DH_DIFF)
    cv_pages = cache_diff_v.reshape(cache_diff_v.shape[0], cache_diff_v.shape[1], PAGE * H_DIFF, DV_DIFF)
    kpe_pages = jnp.swapaxes(cache_mla_kpe, 2, 3)

    pos_p = jnp.maximum(jnp.arange(Lp) - pad, 0)
    pos_s = jnp.tile(jnp.maximum(n_pages * PAGE + jnp.arange(SROWS) - spad, 0), DB)
    tabs_p = _rope_tables(pos_p)
    tabs_s = _rope_tables(pos_s)

    dk_p, dv_p, dk_s, dv_s, ret_p, ret_s = [], [], [], [], [], []
    ckv_p, kpe_p, ckv_s, kpe_s = [], [], [], []
    mc_p, mn_p, mm_p, mc_s, mn_s, mm_s = [], [], [], [], [], []
    cb_p, cb_s = [], []
    for layer in range(depth):
        if layer % 2 == 0:
            e = layer // 2
            li = _lambda_init(layer)
            w_in = w_in_even[e].astype(BF16)
            w_out = w_out_even[e].astype(BF16)
            lam_vecs = jnp.stack([lam_q1[e], lam_k1[e], lam_q2[e], lam_k2[e]])
            zp = norm_matmul(xp, norm_mix[layer], w_in, tmp, 512).reshape(B, Lp, -1)
            zs = norm_matmul(xs, norm_mix[layer], w_in, MS, 512).reshape(DB, SROWS, -1)
            mix_p, st_p = even_prompt_mixer(zp, ret_gn[e], diff_gn[e], lam_vecs, li, pad)
            mix_s, st_s = even_sample_mixer(zs, ck_pages, cv_pages, e, page_table,
                                            state_ret[e].reshape(DB, 2, 128, 128),
                                            ret_gn[e], diff_gn[e], lam_vecs, li, S)
            dk_p.append(zp[:, pad:, 2048:2560].reshape(B, L, H_DIFF, 2 * DH_DIFF))
            dv_p.append(zp[:, pad:, 2560:3072].reshape(B, L, H_DIFF, DV_DIFF))
            dk_s.append(zs[:, spad:, 2048:2560].reshape(DB, S, H_DIFF, 2 * DH_DIFF))
            dv_s.append(zs[:, spad:, 2560:3072].reshape(DB, S, H_DIFF, DV_DIFF))
            ret_p.append(st_p.reshape(B, H_RET, DK_RET, DV_RET))
            ret_s.append(st_s.reshape(DB, H_RET, DK_RET, DV_RET))
        else:
            od = layer // 2
            w_in = _odd_w_in(w_in_odd[od])
            w_out = w_out_odd[od].astype(BF16)
            mw = _mla_weights(mla_g_cq[od], mla_g_ckv[od], mla_w_uq[od], mla_w_uk[od], mla_w_uv[od])
            bvec = jnp.concatenate([jnp.zeros((G_MI,), F32), mlstm_b_i[od], mlstm_b_f[od],
                                    jnp.zeros((LANES - G_MF - H_MLSTM,), F32)]).reshape(1, LANES)
            zp2 = norm_matmul(xp, norm_mix[layer], w_in, tmp, 384)
            zs2 = norm_matmul(xs, norm_mix[layer], w_in, MS, 384)
            ckv_n, kpe_r, q_b, k_b, v_b = mla_prep_prompt(zp2, mw, tabs_p, tmp, tps)
            mix_p, c_p, n_p, m_p = odd_prompt_mixer(
                q_b.reshape(B, Lp, -1), k_b.reshape(B, Lp, -1), v_b.reshape(B, Lp, -1),
                zp2.reshape(B, Lp, -1), bvec, mlstm_gn[od], pad)
            ckv_ns, kpe_rs, qlat, qpe = mla_prep_sample(zs2, mw, tabs_s)
            qlat3 = qlat.reshape(DB, SROWS, H_MLA, D_CKV).transpose(0, 2, 1, 3).reshape(DB, H_MLA * SROWS, D_CKV)
            qpe3 = qpe.reshape(DB, SROWS, H_MLA, D_ROPE).transpose(0, 2, 1, 3).reshape(DB, H_MLA * SROWS, D_ROPE)
            m_in = jnp.pad(state_mlstm_m[od], ((0, 0), (0, LANES - H_MLSTM))).reshape(DB, 1, LANES)
            mix_s, c_s, n_s, m_s = odd_sample_mixer(
                qlat3, qpe3, ckv_ns.reshape(DB, SROWS, -1), kpe_rs.reshape(DB, SROWS, -1),
                zs2.reshape(DB, SROWS, -1), cache_mla_ckv, kpe_pages, od, page_table,
                state_mlstm_c[od], state_mlstm_n[od], m_in, bvec, mlstm_gn[od], mw["wuv"], S)
            ckv_p.append(ckv_n.reshape(B, Lp, -1)[:, pad:])
            kpe_p.append(kpe_r.reshape(B, Lp, -1)[:, pad:])
            ckv_s.append(ckv_ns.reshape(DB, SROWS, -1)[:, spad:])
            kpe_s.append(kpe_rs.reshape(DB, SROWS, -1)[:, spad:])
            mc_p.append(c_p)
            mn_p.append(n_p)
            mm_p.append(m_p[:, 0, :H_MLSTM])
            mc_s.append(c_s)
            mn_s.append(n_s)
            mm_s.append(m_s[:, 0, :H_MLSTM])
        xp = proj_residual(xp, mix_p.reshape(MP, -1), w_out, tmp, Lp, pad)
        xs = proj_residual(xs, mix_s.reshape(MS, -1), w_out, MS, SROWS, spad)
        wg, wu, wd = ffn_w_gate[layer].astype(BF16), ffn_w_up[layer].astype(BF16), ffn_w_down[layer].astype(BF16)
        gstate = jnp.pad(state_ffn_conv[layer], ((0, 0), (SROWS - S - (CONV_W - 1), S), (0, 0))).reshape(MS, F)
        gstate = jnp.pad(gstate, ((HALO, 0), (0, 0)))
        xp, conv_p = conv_ffn(xp, norm_ffn[layer], wg, wu, ffn_conv_w[layer], ffn_conv_b[layer], wd, Lp, rc, tf)
        xs, conv_s = conv_ffn(xs, norm_ffn[layer], wg, wu, ffn_conv_w[layer], ffn_conv_b[layer], wd, MS, MS, tf,
                              gstate=gstate)
        cb_p.append(conv_p)
        cb_s.append(conv_s.reshape(DB, SROWS, F)[:, SROWS - (CONV_W - 1):])

    off = pad + N_META
    xp3 = xp.reshape(B, Lp, D)
    if off % CHUNK == 0 and SEQ % CHUNK == 0:
        y_prompt = final_norm(xp3, norm_final, off, SEQ, CHUNK)
    else:
        y_prompt = final_norm(xp3, norm_final, 0, Lp, CHUNK)[:, off:]
    y_sample = final_norm(xs.reshape(DB, SROWS, D), norm_final, 0, SROWS, SROWS)[:, spad:]

    st = jnp.stack
    return (y_prompt, y_sample,
            st(dk_p), st(dv_p), st(dk_s), st(dv_s),
            st(ckv_p), st(kpe_p), st(ckv_s), st(kpe_s),
            st(ret_p), st(ret_s),
            st(mc_p), st(mn_p), st(mm_p), st(mc_s), st(mn_s), st(mm_s),
            st(cb_p), st(cb_s))
```
